```python
import math
import jax, jax.numpy as jnp
from jax import lax
import numpy as np

D_MODEL = 1024
BATCH = 4
SEQ = 8192
DEPTH = 4

CHUNK = 64
EPS = 1e-6
GDN_HEAD_DIM = 128
GDN_HEADS = D_MODEL // GDN_HEAD_DIM
GDN_W = GDN_HEADS * GDN_HEAD_DIM
GDN_CONV = 4
ATT_HEAD_DIM = 128
ATT_HEADS = D_MODEL // ATT_HEAD_DIM
ATT_W = ATT_HEADS * ATT_HEAD_DIM
PAST_CHUNKS = 8
BAND = (PAST_CHUNKS + 1) * CHUNK
MAX_REL_DIST = 256
D_FF = 4 * D_MODEL
IN_SPLITS = (3 * GDN_W, 4 * GDN_W, 4 * GDN_W + GDN_HEADS, 4 * GDN_W + 2 * GDN_HEADS,
             4 * GDN_W + 2 * GDN_HEADS + 3 * ATT_W)
IN_WIDTH = 4 * GDN_W + 2 * GDN_HEADS + 3 * ATT_W + 2 * D_MODEL

kernel_name = "hybrid_gdn_chunkattn_adaln_encoder"


def rms_norm(x, gain):
    xf = x.astype(jnp.float32)
    y = xf * lax.rsqrt(jnp.mean(xf * xf, axis=-1, keepdims=True) + EPS)
    return (y * gain.astype(jnp.float32)).astype(x.dtype)


def l2_normalize(x):
    xf = x.astype(jnp.float32)
    return xf * lax.rsqrt(jnp.sum(xf * xf, axis=-1, keepdims=True) + EPS)


def causal_depthwise_conv(x, w):
    k_len, ch = w.shape
    return lax.conv_general_dilated(
        x, w[:, None, :], window_strides=(1,), padding=[(k_len - 1, 0)],
        dimension_numbers=('NWC', 'WIO', 'NWC'), feature_group_count=ch)


def chunk_gated_delta_rule(q, k, v, g, beta):
    B, S, H, DK = q.shape
    DV = v.shape[-1]
    N = S // CHUNK
    f32 = jnp.float32

    def to_chunks(t):
        t = t.astype(f32).reshape((B, N, CHUNK, H) + t.shape[3:])
        return jnp.moveaxis(t, 3, 1)

    q = to_chunks(q) * (DK ** -0.5)
    k = to_chunks(k)
    v = to_chunks(v)
    g = to_chunks(g)
    beta = to_chunks(beta)
    G = jnp.cumsum(g, axis=-1)
    pos = jnp.arange(CHUNK)
    incl = pos[:, None] >= pos[None, :]
    strict = pos[:, None] > pos[None, :]
    decay = jnp.exp(jnp.where(incl, G[..., :, None] - G[..., None, :], -jnp.inf))
    kk = jnp.einsum('bhnid,bhnjd->bhnij', k * beta[..., None], k)
    a_mat = jnp.where(strict, kk * decay, 0.0) + jnp.eye(CHUNK, dtype=f32)
    rhs = jnp.concatenate([v * beta[..., None], k * (beta * jnp.exp(G))[..., None]], axis=-1)
    sol = lax.linalg.triangular_solve(a_mat, rhs, left_side=True, lower=True, unit_diagonal=True)
    u, w = sol[..., :DV], sol[..., DV:]
    attn = jnp.einsum('bhnid,bhnjd->bhnij', q, k) * decay
    q_dec = q * jnp.exp(G)[..., None]
    k_dec = k * jnp.exp(G[..., -1:] - G)[..., None]
    chunk_decay = jnp.exp(G[..., -1])
    xs = tuple(jnp.moveaxis(t, 2, 0) for t in (q_dec, k_dec, u, w, attn, chunk_decay))

    def step(state, inp):
        qd, kd, uu, ww, at, cd = inp
        v_new = uu - jnp.einsum('bhck,bhkv->bhcv', ww, state)
        o = jnp.einsum('bhck,bhkv->bhcv', qd, state) + jnp.einsum('bhij,bhjv->bhiv', at, v_new)
        state = state * cd[..., None, None] + jnp.einsum('bhck,bhcv->bhkv', kd, v_new)
        return state, o

    s0 = jnp.zeros((B, H, DK, DV), f32)
    _, o = lax.scan(step, s0, xs)
    return jnp.transpose(o, (1, 0, 3, 2, 4)).reshape(B, S, H, DV)


def gated_deltanet_branch(qkv, gate, a, b, conv_w, a_log, dt_bias, o_norm):
    B, S, _ = qkv.shape
    qkv = jax.nn.silu(causal_depthwise_conv(qkv, conv_w))
    q, k, v = jnp.split(qkv, 3, axis=-1)
    q = l2_normalize(q.reshape(B, S, GDN_HEADS, GDN_HEAD_DIM))
    k = l2_normalize(k.reshape(B, S, GDN_HEADS, GDN_HEAD_DIM))
    v = v.reshape(B, S, GDN_HEADS, GDN_HEAD_DIM)
    beta = jax.nn.sigmoid(b.astype(jnp.float32))
    g = -jnp.exp(a_log.astype(jnp.float32)) * jax.nn.softplus(
        a.astype(jnp.float32) + dt_bias.astype(jnp.float32))
    o = chunk_gated_delta_rule(q, k, v, g, beta)
    o = rms_norm(o, o_norm) * jax.nn.silu(
        gate.reshape(B, S, GDN_HEADS, GDN_HEAD_DIM).astype(jnp.float32))
    return o.reshape(B, S, GDN_W).astype(qkv.dtype)


def chunk_band_attention(q, k, v, rel_bias):
    B, S, _ = q.shape
    N = S // CHUNK
    dtype = q.dtype
    q = q.reshape(B, N, CHUNK, ATT_HEADS, ATT_HEAD_DIM).transpose(1, 0, 2, 3, 4) * (ATT_HEAD_DIM ** -0.5)
    pad = ((0, 0), (PAST_CHUNKS * CHUNK, 0), (0, 0), (0, 0))
    k = jnp.pad(k.reshape(B, S, ATT_HEADS, ATT_HEAD_DIM), pad)
    v = jnp.pad(v.reshape(B, S, ATT_HEADS, ATT_HEAD_DIM), pad)
    qi = jnp.arange(CHUNK)[:, None]
    kj = jnp.arange(BAND)[None, :]
    rel = jnp.clip(qi - kj + PAST_CHUNKS * CHUNK, -MAX_REL_DIST, MAX_REL_DIST) + MAX_REL_DIST
    bias = rel_bias[:, rel].astype(jnp.float32)

    def one_chunk(args):
        n, qn = args
        kb = lax.dynamic_slice_in_dim(k, n * CHUNK, BAND, axis=1)
        vb = lax.dynamic_slice_in_dim(v, n * CHUNK, BAND, axis=1)
        s = jnp.einsum('bqhd,bkhd->bhqk', qn, kb).astype(jnp.float32) + bias
        valid = kj >= (PAST_CHUNKS - n) * CHUNK
        s = jnp.where(valid, s, -jnp.inf)
        p = jax.nn.softmax(s, axis=-1).astype(dtype)
        return jnp.einsum('bhqk,bkhd->bqhd', p, vb)

    o = lax.map(one_chunk, (jnp.arange(N), q))
    return o.transpose(1, 0, 2, 3, 4).reshape(B, S, ATT_W)


def setup_inputs(seed: int = 0) -> dict:
    key = jax.random.key(seed)
    ks = jax.random.split(key, 16)
    f32 = jnp.float32
    L = DEPTH

    def nrm(k, shape, s):
        return jax.random.normal(k, shape, f32) * s

    x = nrm(ks[0], (BATCH, SEQ, D_MODEL), 1.0)
    c = nrm(ks[1], (BATCH, D_MODEL), 1.0)
    w_ada = nrm(ks[2], (L, D_MODEL, 6 * D_MODEL), 0.5 * D_MODEL ** -0.5)
    b_ada = nrm(ks[3], (L, 6 * D_MODEL), 0.02)
    norm_mix = 1.0 + nrm(ks[4], (L, D_MODEL), 0.02)
    norm_mlp = 1.0 + nrm(ks[5], (L, D_MODEL), 0.02)
    w_in = nrm(ks[6], (L, D_MODEL, IN_WIDTH), D_MODEL ** -0.5)
    conv_w = nrm(ks[7], (L, GDN_CONV, 3 * GDN_W), GDN_CONV ** -0.5)
    a_log = jnp.log(jax.random.uniform(ks[8], (L, GDN_HEADS), f32, 1.0, 16.0))
    dt = jnp.exp(jax.random.uniform(ks[9], (L, GDN_HEADS), f32, math.log(1e-3), math.log(1e-1)))
    dt_bias = dt + jnp.log(-jnp.expm1(-dt))
    gdn_norm = 1.0 + nrm(ks[10], (L, GDN_HEAD_DIM), 0.02)
    rel_bias = nrm(ks[11], (L, ATT_HEADS, 2 * MAX_REL_DIST + 1), 0.2)
    w_out = nrm(ks[12], (L, D_MODEL, D_MODEL), D_MODEL ** -0.5)
    w_ff_in = nrm(ks[13], (L, D_MODEL, D_FF), D_MODEL ** -0.5)
    w_ff_out = nrm(ks[14], (L, D_FF, D_MODEL), D_FF ** -0.5)
    final_norm = 1.0 + nrm(ks[15], (D_MODEL,), 0.02)
    return {"x": x, "c": c, "w_ada": w_ada, "b_ada": b_ada, "norm_mix": norm_mix,
            "norm_mlp": norm_mlp, "w_in": w_in, "conv_w": conv_w, "a_log": a_log,
            "dt_bias": dt_bias, "gdn_norm": gdn_norm, "rel_bias": rel_bias, "w_out": w_out,
            "w_ff_in": w_ff_in, "w_ff_out": w_ff_out, "final_norm": final_norm}


def reference(x, c, w_ada, b_ada, norm_mix, norm_mlp, w_in, conv_w, a_log, dt_bias,
              gdn_norm, rel_bias, w_out, w_ff_in, w_ff_out, final_norm):
    c_act = jax.nn.silu(c)
    for l in range(DEPTH):
        mod = (c_act @ w_ada[l] + b_ada[l])[:, None, :]
        sh1, sc1, gt1, sh2, sc2, gt2 = jnp.split(mod, 6, axis=-1)
        h = rms_norm(x, norm_mix[l]) * (1.0 + sc1) + sh1
        z = h @ w_in[l]
        gdn_qkv, gdn_gate, gdn_a, gdn_b, att_qkv, br_gates = jnp.split(z, IN_SPLITS, axis=-1)
        o_a = gated_deltanet_branch(gdn_qkv, gdn_gate, gdn_a, gdn_b, conv_w[l],
                                    a_log[l], dt_bias[l], gdn_norm[l])
        att_q, att_k, att_v = jnp.split(att_qkv, 3, axis=-1)
        o_b = chunk_band_attention(att_q, att_k, att_v, rel_bias[l])
        g_a, g_b = jnp.split(jax.nn.sigmoid(br_gates), 2, axis=-1)
        x = x + gt1 * ((g_a * o_a + g_b * o_b) @ w_out[l])
        h = rms_norm(x, norm_mlp[l]) * (1.0 + sc2) + sh2
        x = x + gt2 * (jnp.square(jax.nn.relu(h @ w_ff_in[l])) @ w_ff_out[l])
    return rms_norm(x, final_norm)
```

```python
import functools
import math

import jax
import jax.numpy as jnp
from jax import lax
from jax.experimental import pallas as pl
from jax.experimental.pallas import tpu as pltpu

LANES = 128
CHUNK = 64
HEAD_DIM = 128
CONV_K = 4
PAST_CHUNKS = 8
MAX_REL_DIST = 256
EPS = 1e-6
VMEM_LIMIT_BYTES = 56 * 1024 * 1024
NEG_BIG = -1e30

F32 = jnp.float32
BF16 = jnp.bfloat16
NT_DIMS = (((1,), (1,)), ((), ()))
TN_DIMS = (((0,), (0,)), ((), ()))


def _params(*semantics):
    return pltpu.CompilerParams(dimension_semantics=semantics,
                                vmem_limit_bytes=VMEM_LIMIT_BYTES)


def _resident(shape):
    return pl.BlockSpec(shape, lambda *_: (0,) * len(shape), pipeline_mode=pl.Buffered(1))


def _dot(a, b):
    return jnp.dot(a.astype(BF16), b.astype(BF16), preferred_element_type=F32)


def _silu(x):
    return x * jax.nn.sigmoid(x)


def _rms_modulate(x, gain, scale, shift):
    y = x * lax.rsqrt(jnp.mean(x * x, axis=-1, keepdims=True) + EPS)
    return (y * gain) * (1.0 + scale) + shift


def _ada_kernel(c_ref, w_ref, b_ref, o_ref):
    c = c_ref[...]
    o_ref[0] = jnp.dot(_silu(c), w_ref[0], preferred_element_type=F32,
                       precision=lax.Precision.HIGHEST) + b_ref[0]


def _ada_table(c, w_ada, b_ada):
    n_layers, d, d6 = w_ada.shape
    rows = -(-c.shape[0] // 8) * 8
    c_pad = jnp.zeros((rows, d), F32).at[:c.shape[0]].set(c)
    tn = d6 // 4
    return pl.pallas_call(
        _ada_kernel,
        grid=(n_layers, d6 // tn),
        in_specs=[pl.BlockSpec((rows, d), lambda l, j: (0, 0)),
                  pl.BlockSpec((1, d, tn), lambda l, j: (l, 0, j)),
                  pl.BlockSpec((1, 1, tn), lambda l, j: (l, 0, j))],
        out_specs=pl.BlockSpec((1, rows, tn), lambda l, j: (l, 0, j)),
        out_shape=jax.ShapeDtypeStruct((n_layers, rows, d6), F32),
        compiler_params=_params("parallel", "parallel"),
        name="ada_table",
    )(c_pad, w_ada, b_ada.reshape(n_layers, 1, d6))


IN_PROJ_ROWS = 512
IN_PROJ_COLS = 512


def _in_proj_kernel(x_ref, sc_ref, sh_ref, g_ref, wgq_ref, wgg_ref, wab_ref, waq_ref, wbr_ref,
                    gq_ref, gg_ref, abt_ref, aq_ref, br_ref):
    h = _rms_modulate(x_ref[0], g_ref[...], sc_ref[0], sh_ref[0]).astype(BF16)
    for w_ref, o_ref in ((wgq_ref, gq_ref), (wgg_ref, gg_ref), (waq_ref, aq_ref), (wbr_ref, br_ref)):
        for c0 in range(0, w_ref.shape[1], IN_PROJ_COLS):
            cols = slice(c0, c0 + IN_PROJ_COLS)
            o_ref[0, :, cols] = jnp.dot(h, w_ref[:, cols], preferred_element_type=F32).astype(o_ref.dtype)
    abt_ref[0] = lax.dot_general(wab_ref[...], h, NT_DIMS, preferred_element_type=F32)


def _in_proj(x, scale, shift, gain, wgq, wgg, wab_t, waq, wbr):
    b, s, d = x.shape
    tm = min(IN_PROJ_ROWS, s)
    row = lambda n: pl.BlockSpec((1, tm, n), lambda i, j: (i, j, 0))
    vec = pl.BlockSpec((1, 1, d), lambda i, j: (i, 0, 0))
    outs = (jax.ShapeDtypeStruct((b, s, wgq.shape[1]), BF16),
            jax.ShapeDtypeStruct((b, s, wgg.shape[1]), BF16),
            jax.ShapeDtypeStruct((b, wab_t.shape[0], s), F32),
            jax.ShapeDtypeStruct((b, s, waq.shape[1]), BF16),
            jax.ShapeDtypeStruct((b, s, wbr.shape[1]), BF16))
    return pl.pallas_call(
        _in_proj_kernel,
        grid=(b, s // tm),
        in_specs=[row(d), vec, vec, _resident((1, d)), _resident(wgq.shape), _resident(wgg.shape),
                  _resident(wab_t.shape), _resident(waq.shape), _resident(wbr.shape)],
        out_specs=(row(wgq.shape[1]), row(wgg.shape[1]),
                   pl.BlockSpec((1, wab_t.shape[0], tm), lambda i, j: (i, 0, j)),
                   row(waq.shape[1]), row(wbr.shape[1])),
        out_shape=outs,
        compiler_params=_params("parallel", "parallel"),
        name="in_proj",
    )(x, scale, shift, gain, wgq, wgg, wab_t, waq, wbr)


GATE_ROWS = 512
COL_G, COL_BETA, COL_EG, COL_BEG, COL_EGL, COL_CD = 0, 8, 16, 24, 32, 40


def _gates_kernel(abt_ref, alog_ref, dtb_ref, rowg_ref, col_ref):
    n_heads = alog_ref.shape[0]
    r = abt_ref.shape[2]
    ab = abt_ref[0]
    a, bl = ab[:n_heads], ab[n_heads:]
    z = a + dtb_ref[...]
    softplus = jnp.maximum(z, 0.0) + jnp.log1p(jnp.exp(-jnp.abs(z)))
    g = -jnp.exp(alog_ref[...]) * softplus
    beta = jax.nn.sigmoid(bl)
    m = lax.broadcasted_iota(jnp.int32, (r, r), 0)
    j = lax.broadcasted_iota(jnp.int32, (r, r), 1)
    same = (m // CHUNK) == (j // CHUNK)
    upper = jnp.where(same & (m <= j), 1.0, 0.0).astype(F32)
    ones = jnp.where(same, 1.0, 0.0).astype(F32)
    hi = lax.Precision.HIGHEST
    gc = jnp.dot(g, upper, preferred_element_type=F32, precision=hi)
    gl = jnp.dot(g, ones, preferred_element_type=F32, precision=hi)
    m2 = lax.broadcasted_iota(jnp.int32, (r, 2 * r), 0)
    j2 = lax.broadcasted_iota(jnp.int32, (r, 2 * r), 1)
    upper2 = jnp.where(((m2 // CHUNK) == (j2 // LANES)) & ((m2 % CHUNK) <= (j2 % CHUNK)), 1.0, 0.0)
    rowg_ref[0] = jnp.dot(g, upper2.astype(F32), preferred_element_type=F32, precision=hi)
    eg = jnp.exp(gc)
    pack = jnp.concatenate(
        [gc, beta, eg, beta * eg, jnp.exp(gl - gc), jnp.exp(gl),
         jnp.zeros((LANES - 6 * n_heads, r), F32)], axis=0)
    for t in range(r // LANES):
        col_ref[0, t * LANES:(t + 1) * LANES, :] = pack[:, t * LANES:(t + 1) * LANES].T


def _gates(abt, a_log, dt_bias):
    b, h2, s = abt.shape
    n_heads = h2 // 2
    r = min(GATE_ROWS, s)
    return pl.pallas_call(
        _gates_kernel,
        grid=(b, s // r),
        in_specs=[pl.BlockSpec((1, h2, r), lambda i, j: (i, 0, j)),
                  pl.BlockSpec((n_heads, 1), lambda i, j: (0, 0)),
                  pl.BlockSpec((n_heads, 1), lambda i, j: (0, 0))],
        out_specs=(pl.BlockSpec((1, n_heads, 2 * r), lambda i, j: (i, 0, j)),
                   pl.BlockSpec((1, r, LANES), lambda i, j: (i, j, 0))),
        out_shape=(jax.ShapeDtypeStruct((b, n_heads, 2 * s), F32),
                   jax.ShapeDtypeStruct((b, s, LANES), F32)),
        compiler_params=_params("parallel", "parallel"),
        name="gdn_gates",
    )(abt, a_log.reshape(n_heads, 1), dt_bias.reshape(n_heads, 1))


GDN_ROWS = 256
TAIL = 8


def _gdn_kernel(qkv_ref, gate_ref, col_ref, rowg_ref, cw_ref, on_ref, o_ref,
                state_ref, ext_ref):
    r = qkv_ref.shape[1]
    d = gate_ref.shape[2]
    n_heads = d // HEAD_DIM

    @pl.when(pl.program_id(1) == 0)
    def _():
        state_ref[...] = jnp.zeros_like(state_ref)
        ext_ref[0:TAIL, :] = jnp.zeros((TAIL, ext_ref.shape[1]), F32)

    ext_ref[TAIL:TAIL + r, :] = qkv_ref[0].astype(F32)

    row = lax.broadcasted_iota(jnp.int32, (CHUNK, LANES), 0)
    lane = lax.broadcasted_iota(jnp.int32, (CHUNK, LANES), 1)
    jl = lane % CHUNK
    low = lane < CHUNK
    eye_hi = jnp.where(lane - CHUNK == row, 1.0, 0.0).astype(F32)
    zeros_cl = jnp.zeros((CHUNK, LANES), F32)
    scale = HEAD_DIM ** -0.5

    def conv_silu(c0):
        acc = None
        for j in range(CONV_K):
            start = TAIL - (CONV_K - 1) + j
            term = ext_ref[start:start + r, c0:c0 + HEAD_DIM] * cw_ref[j:j + 1, c0:c0 + HEAD_DIM]
            acc = term if acc is None else acc + term
        return _silu(acc)

    def l2n(t):
        return t * lax.rsqrt(jnp.sum(t * t, axis=-1, keepdims=True) + EPS)

    for h in range(n_heads):
        q = l2n(conv_silu(h * HEAD_DIM)) * scale
        k = l2n(conv_silu(d + h * HEAD_DIM))
        v = conv_silu(2 * d + h * HEAD_DIM)
        cols = col_ref[0]
        col = lambda base: cols[:, base + h:base + h + 1]
        g_col, beta, eg, beg, egl, cd = (col(COL_G), col(COL_BETA), col(COL_EG),
                                         col(COL_BEG), col(COL_EGL), col(COL_CD))
        kb = k * beta
        rhs = jnp.concatenate([v * beta, k * beg], axis=1)
        qd = q * eg
        kd = k * egl
        state = state_ref[h]
        outs = []
        for c in range(r // CHUNK):
            rows = slice(c * CHUNK, (c + 1) * CHUNK)
            k_pad = jnp.concatenate([k[rows], zeros_cl], axis=0)
            sc = lax.dot_general(jnp.concatenate([q[rows], kb[rows]], axis=0).astype(BF16),
                                 k_pad.astype(BF16), NT_DIMS, preferred_element_type=F32)
            g_row = rowg_ref[0, h:h + 1, 2 * c * CHUNK:(2 * c + 2) * CHUNK]
            diff = g_col[rows] - g_row
            dec = jnp.exp(jnp.where(row >= jl, diff, -jnp.inf))
            attn = sc[:CHUNK] * dec
            a_mat = jnp.where(row > jl, sc[CHUNK:] * dec, 0.0)
            tb = jnp.where(low, -a_mat, eye_hi)
            for _ in range(int(math.log2(CHUNK))):
                hi = tb.astype(BF16)
                lo = (tb - hi.astype(F32)).astype(BF16)
                p1 = jnp.dot(hi[:, :CHUNK], jnp.concatenate([hi, lo], axis=1), preferred_element_type=F32)
                p2 = jnp.dot(lo[:, :CHUNK], hi, preferred_element_type=F32)
                upd = p1[:, :LANES] + p1[:, LANES:] + p2
                tb = jnp.where(low, upd, tb + upd)
            uw = _dot(tb, jnp.concatenate([jnp.zeros((CHUNK, 2 * HEAD_DIM), F32), rhs[rows]], axis=0))
            u, w = uw[:, :HEAD_DIM], uw[:, HEAD_DIM:]
            ws_qs = _dot(jnp.concatenate([w, qd[rows]], axis=0), state)
            v_new = u - ws_qs[:CHUNK]
            outs.append(ws_qs[CHUNK:] + _dot(attn[:, :CHUNK], v_new))
            state = state * cd[c * CHUNK:c * CHUNK + 1] + lax.dot_general(
                kd[rows].astype(BF16), v_new.astype(BF16), TN_DIMS, preferred_element_type=F32)
        state_ref[h] = state
        o = jnp.concatenate(outs, axis=0)
        o = o * lax.rsqrt(jnp.mean(o * o, axis=-1, keepdims=True) + EPS) * on_ref[...]
        hs = slice(h * HEAD_DIM, (h + 1) * HEAD_DIM)
        o_ref[0, :, hs] = (o * _silu(gate_ref[0, :, hs].astype(F32))).astype(o_ref.dtype)

    ext_ref[0:TAIL, :] = ext_ref[r:r + TAIL, :]


def _gdn(zg_qkv, zg_gate, col_tab, row_g, conv_w, o_norm):
    b, s, d3 = zg_qkv.shape
    d = d3 // 3
    n_heads = d // HEAD_DIM
    r = min(GDN_ROWS, s)
    return pl.pallas_call(
        _gdn_kernel,
        grid=(b, s // r),
        in_specs=[pl.BlockSpec((1, r, d3), lambda i, j: (i, j, 0)),
                  pl.BlockSpec((1, r, d), lambda i, j: (i, j, 0)),
                  pl.BlockSpec((1, r, LANES), lambda i, j: (i, j, 0)),
                  pl.BlockSpec((1, n_heads, 2 * r), lambda i, j: (i, 0, j)),
                  pl.BlockSpec((CONV_K, d3), lambda i, j: (0, 0)),
                  pl.BlockSpec((1, HEAD_DIM), lambda i, j: (0, 0))],
        out_specs=pl.BlockSpec((1, r, d), lambda i, j: (i, j, 0)),
        out_shape=jax.ShapeDtypeStruct((b, s, d), BF16),
        scratch_shapes=[pltpu.VMEM((n_heads, HEAD_DIM, HEAD_DIM), F32),
                        pltpu.VMEM((r + TAIL, d3), F32)],
        compiler_params=_params("parallel", "arbitrary"),
        name="gdn",
    )(zg_qkv, zg_gate, col_tab, row_g, conv_w, o_norm.reshape(1, HEAD_DIM))


ATT_ROWS = 512
ATT_SUB = 2 * CHUNK
ATT_KEYS = ATT_SUB + PAST_CHUNKS * CHUNK


def _band_bias(rel_bias):
    qi = jnp.arange(ATT_SUB)[:, None]
    kj = jnp.arange(ATT_KEYS)[None, :]
    band = (kj // CHUNK - qi // CHUNK) * CHUNK + kj % CHUNK
    rel = jnp.clip(qi % CHUNK - band + PAST_CHUNKS * CHUNK, -MAX_REL_DIST, MAX_REL_DIST) + MAX_REL_DIST
    inside = (band >= 0) & (band < (PAST_CHUNKS + 1) * CHUNK)
    return jnp.where(inside[None], rel_bias[:, rel].astype(F32), NEG_BIG)


def _attn_kernel(q_ref, kp_ref, kc_ref, vp_ref, vc_ref, bias_ref, o_ref):
    first = pl.program_id(2) == 0
    rq = q_ref.shape[1]
    k_all = jnp.concatenate([kp_ref[0], kc_ref[0]], axis=0)
    v_all = jnp.concatenate([vp_ref[0], vc_ref[0]], axis=0)
    key_pos = lax.broadcasted_iota(jnp.int32, (ATT_SUB, ATT_KEYS), 1)
    scale = HEAD_DIM ** -0.5
    for t in range(rq // ATT_SUB):
        k0 = rq - PAST_CHUNKS * CHUNK + t * ATT_SUB
        keys = slice(k0, k0 + ATT_KEYS)
        s = lax.dot_general(q_ref[0, t * ATT_SUB:(t + 1) * ATT_SUB, :], k_all[keys], NT_DIMS,
                            preferred_element_type=F32) * scale + bias_ref[0]
        s = jnp.where(first & (key_pos < rq - k0), NEG_BIG, s)
        p = jnp.exp(s - jnp.max(s, axis=-1, keepdims=True))
        l = jnp.sum(p, axis=-1, keepdims=True)
        o = jnp.dot(p.astype(BF16), v_all[keys], preferred_element_type=F32)
        o_ref[0, t * ATT_SUB:(t + 1) * ATT_SUB, :] = (o / l).astype(o_ref.dtype)


def _attention(za_qkv, bias_tab):
    b, s, d3 = za_qkv.shape
    d = d3 // 3
    n_heads = d // HEAD_DIM
    rq = min(ATT_ROWS, s)
    assert rq >= PAST_CHUNKS * CHUNK and rq % ATT_SUB == 0
    cur = lambda off: pl.BlockSpec((1, rq, HEAD_DIM), lambda i, h, j: (i, j, off * n_heads + h))
    prev = lambda off: pl.BlockSpec((1, rq, HEAD_DIM),
                                    lambda i, h, j: (i, jnp.maximum(j - 1, 0), off * n_heads + h))
    return pl.pallas_call(
        _attn_kernel,
        grid=(b, n_heads, s // rq),
        in_specs=[cur(0), prev(1), cur(1), prev(2), cur(2),
                  pl.BlockSpec((1, ATT_SUB, ATT_KEYS), lambda i, h, j: (h, 0, 0))],
        out_specs=pl.BlockSpec((1, rq, HEAD_DIM), lambda i, h, j: (i, j, h)),
        out_shape=jax.ShapeDtypeStruct((b, s, d), BF16),
        compiler_params=_params("parallel", "parallel", "arbitrary"),
        name="band_attn",
    )(za_qkv, za_qkv, za_qkv, za_qkv, za_qkv, bias_tab)


POST_ROWS = 512
FF_COLS = 1024


def _post_kernel(x_ref, oa_ref, ob_ref, br_ref, g1_ref, sh_ref, sc_ref, g2_ref, gain_ref,
                 wo_ref, w1_ref, w2_ref, fin_ref, o_ref, *, final):
    d = x_ref.shape[2]
    ga = jax.nn.sigmoid(br_ref[0, :, :d].astype(F32))
    gb = jax.nn.sigmoid(br_ref[0, :, d:].astype(F32))
    merged = ga * oa_ref[0].astype(F32) + gb * ob_ref[0].astype(F32)
    x = x_ref[0] + g1_ref[0] * jnp.dot(merged.astype(BF16), wo_ref[...], preferred_element_type=F32)
    h = _rms_modulate(x, gain_ref[...], sc_ref[0], sh_ref[0]).astype(BF16)
    acc = None
    for c0 in range(0, w1_ref.shape[1], FF_COLS):
        a = jnp.dot(h, w1_ref[:, c0:c0 + FF_COLS], preferred_element_type=F32)
        a = jnp.square(jnp.maximum(a, 0.0)).astype(BF16)
        part = jnp.dot(a, w2_ref[c0:c0 + FF_COLS, :], preferred_element_type=F32)
        acc = part if acc is None else acc + part
    x = x + g2_ref[0] * acc
    if final:
        x = x * lax.rsqrt(jnp.mean(x * x, axis=-1, keepdims=True) + EPS) * fin_ref[...]
    o_ref[0] = x


def _post(x, o_a, o_b, br, gate1, shift2, scale2, gate2, gain, w_out, w1, w2, fin_gain, final):
    b, s, d = x.shape
    tm = min(POST_ROWS, s)
    row = lambda n: pl.BlockSpec((1, tm, n), lambda i, j: (i, j, 0))
    vec = pl.BlockSpec((1, 1, d), lambda i, j: (i, 0, 0))
    return pl.pallas_call(
        functools.partial(_post_kernel, final=final),
        grid=(b, s // tm),
        in_specs=[row(d), row(d), row(d), row(2 * d), vec, vec, vec, vec, _resident((1, d)),
                  _resident(w_out.shape), _resident(w1.shape), _resident(w2.shape), _resident((1, d))],
        out_specs=row(d),
        out_shape=jax.ShapeDtypeStruct((b, s, d), F32),
        compiler_params=_params("parallel", "parallel"),
        name="post_final" if final else "post",
    )(x, o_a, o_b, br, gate1, shift2, scale2, gate2, gain, w_out, w1, w2, fin_gain)


def kernel(x, c, w_ada, b_ada, norm_mix, norm_mlp, w_in, conv_w, a_log, dt_bias, gdn_norm, rel_bias,
           w_out, w_ff_in, w_ff_out, final_norm):
    b, s, d = x.shape
    n_layers = w_ada.shape[0]
    n_heads = d // HEAD_DIM
    assert s % CHUNK == 0 and d % HEAD_DIM == 0

    mod = _ada_table(c, w_ada, b_ada)[:, :b]
    o_gate, o_a, o_b, o_att, o_br = 3 * d, 4 * d, 4 * d + n_heads, 4 * d + 2 * n_heads, 7 * d + 2 * n_heads

    for l in range(n_layers):
        sh1, sc1, gt1, sh2, sc2, gt2 = (m[:, None, :] for m in jnp.split(mod[l], 6, axis=-1))
        wl = w_in[l]
        z = _in_proj(x, sc1, sh1, norm_mix[l].reshape(1, d),
                     wl[:, :o_gate].astype(BF16), wl[:, o_gate:o_a].astype(BF16),
                     wl[:, o_a:o_att].T.astype(BF16), wl[:, o_att:o_br].astype(BF16),
                     wl[:, o_br:].astype(BF16))
        zg_qkv, zg_gate, abt, za_qkv, z_br = z
        row_g, col_tab = _gates(abt, a_log[l], dt_bias[l])
        oa = _gdn(zg_qkv, zg_gate, col_tab, row_g, conv_w[l], gdn_norm[l])
        ob = _attention(za_qkv, _band_bias(rel_bias[l]))
        x = _post(x, oa, ob, z_br, gt1, sh2, sc2, gt2, norm_mlp[l].reshape(1, d),
                  w_out[l].astype(BF16), w_ff_in[l].astype(BF16), w_ff_out[l].astype(BF16),
                  final_norm.reshape(1, d), final=(l == n_layers - 1))
    return x
```

```python
import functools
import math

import jax
import jax.numpy as jnp
from jax import lax
from jax.experimental import pallas as pl
from jax.experimental.pallas import tpu as pltpu

LANES = 128
SUBLANES = 8
CHUNK = 64
HEAD_DIM = 128
CONV_K = 4
PAST_CHUNKS = 8
MAX_REL_DIST = 256
EPS = 1e-6
VMEM_LIMIT_BYTES = 56 * 1024 * 1024
NEG_BIG = -1e30

F32 = jnp.float32
BF16 = jnp.bfloat16
NT_DIMS = (((1,), (1,)), ((), ()))
TN_DIMS = (((0,), (0,)), ((), ()))


def _params(*semantics):
    return pltpu.CompilerParams(dimension_semantics=semantics,
                                vmem_limit_bytes=VMEM_LIMIT_BYTES)


def _resident(shape):
    return pl.BlockSpec(shape, lambda *_: (0,) * len(shape), pipeline_mode=pl.Buffered(1))


def _dot(a, b):
    return jnp.dot(a.astype(BF16), b.astype(BF16), preferred_element_type=F32)


def _silu(x):
    return x * jax.nn.sigmoid(x)


def _rms_modulate(x, gain, scale, shift):
    y = x * lax.rsqrt(jnp.mean(x * x, axis=-1, keepdims=True) + EPS)
    return (y * gain) * (1.0 + scale) + shift


def _ada_kernel(c_ref, w_ref, b_ref, o_ref):
    c = c_ref[...]
    o_ref[0] = jnp.dot(_silu(c), w_ref[0], preferred_element_type=F32,
                       precision=lax.Precision.HIGHEST) + b_ref[0]


def _ada_table(c, w_ada, b_ada):
    n_layers, d, d6 = w_ada.shape
    rows = -(-c.shape[0] // SUBLANES) * SUBLANES
    c_pad = jnp.zeros((rows, d), F32).at[:c.shape[0]].set(c)
    tn = d6 // 4
    return pl.pallas_call(
        _ada_kernel,
        grid=(n_layers, d6 // tn),
        in_specs=[pl.BlockSpec((rows, d), lambda l, j: (0, 0)),
                  pl.BlockSpec((1, d, tn), lambda l, j: (l, 0, j)),
                  pl.BlockSpec((1, 1, tn), lambda l, j: (l, 0, j))],
        out_specs=pl.BlockSpec((1, rows, tn), lambda l, j: (l, 0, j)),
        out_shape=jax.ShapeDtypeStruct((n_layers, rows, d6), F32),
        compiler_params=_params("parallel", "parallel"),
        name="ada_table",
    )(c_pad, w_ada, b_ada.reshape(n_layers, 1, d6))


IN_PROJ_ROWS = 512
IN_PROJ_COLS = 512
TAIL = SUBLANES
CONV_ROWS = 128


def _in_proj_kernel(x_ref, sc_ref, sh_ref, g_ref, wgq_ref, wgg_ref, wab_ref, waq_ref, wbr_ref, cw_ref,
                    gq_ref, gg_ref, abt_ref, aq_ref, br_ref, zc_ref, tail_ref):
    tm = x_ref.shape[1]
    d = x_ref.shape[2]
    heads_per_chunk = IN_PROJ_COLS // HEAD_DIM

    @pl.when(pl.program_id(1) == 0)
    def _():
        tail_ref[...] = jnp.zeros_like(tail_ref)

    h = _rms_modulate(x_ref[0], g_ref[...], sc_ref[0], sh_ref[0]).astype(BF16)

    def conv_epilogue(zc, ci, c0):
        zc[0:TAIL, :] = tail_ref[ci]
        tail_ref[ci] = zc[tm:tm + TAIL, :]
        for i in range(heads_per_chunk):
            lanes = slice(i * HEAD_DIM, (i + 1) * HEAD_DIM)
            w_lanes = slice(c0 + i * HEAD_DIM, c0 + (i + 1) * HEAD_DIM)
            for r0 in range(0, tm, CONV_ROWS):
                acc = None
                for j in range(CONV_K):
                    start = TAIL - (CONV_K - 1) + j + r0
                    term = zc[start:start + CONV_ROWS, lanes] * cw_ref[j:j + 1, w_lanes]
                    acc = term if acc is None else acc + term
                t = _silu(acc)
                if c0 < 2 * d:
                    t = t * lax.rsqrt(jnp.sum(t * t, axis=-1, keepdims=True) + EPS)
                if c0 < d:
                    t = t * (HEAD_DIM ** -0.5)
                gq_ref[0, ci * heads_per_chunk + i, r0:r0 + CONV_ROWS, :] = t.astype(gq_ref.dtype)

    def heads_epilogue(zc, o_ref, ci):
        for i in range(heads_per_chunk):
            o_ref[0, ci * heads_per_chunk + i] = (
                zc[TAIL:TAIL + tm, i * HEAD_DIM:(i + 1) * HEAD_DIM].astype(o_ref.dtype))

    def rows_epilogue(zc, c0):
        br_ref[0, :, c0:c0 + IN_PROJ_COLS] = zc[TAIL:TAIL + tm, :].astype(br_ref.dtype)

    heavy = [(wgq_ref, c0, functools.partial(conv_epilogue, ci=ci, c0=c0))
             for ci, c0 in enumerate(range(0, wgq_ref.shape[1], IN_PROJ_COLS))]
    light = [(w_ref, c0, functools.partial(heads_epilogue, o_ref=o_ref, ci=ci))
             for w_ref, o_ref in ((wgg_ref, gg_ref), (waq_ref, aq_ref))
             for ci, c0 in enumerate(range(0, w_ref.shape[1], IN_PROJ_COLS))]
    light += [(wbr_ref, c0, functools.partial(rows_epilogue, c0=c0))
              for c0 in range(0, wbr_ref.shape[1], IN_PROJ_COLS)]
    jobs = [job for pair in zip(heavy, light) for job in pair] + light[len(heavy):]

    def product(k):
        w_ref, c0, _ = jobs[k]
        zc_ref[k % 2, TAIL:TAIL + tm, :] = jnp.dot(h, w_ref[:, c0:c0 + IN_PROJ_COLS],
                                                   preferred_element_type=F32)

    product(0)
    for k in range(len(jobs)):
        if k + 1 < len(jobs):
            product(k + 1)
        jobs[k][2](zc_ref.at[k % 2])
    abt_ref[0] = lax.dot_general(wab_ref[...], h, NT_DIMS, preferred_element_type=F32)


def _in_proj(x, scale, shift, gain, wgq, wgg, wab_t, waq, wbr, conv_w):
    b, s, d = x.shape
    tm = min(IN_PROJ_ROWS, s)
    n_heads = d // HEAD_DIM
    row = lambda n: pl.BlockSpec((1, tm, n), lambda i, j: (i, j, 0))
    heads = lambda n: pl.BlockSpec((1, n, tm, HEAD_DIM), lambda i, j: (i, 0, j, 0))
    vec = pl.BlockSpec((1, 1, d), lambda i, j: (i, 0, 0))
    outs = (jax.ShapeDtypeStruct((b, 3 * n_heads, s, HEAD_DIM), BF16),
            jax.ShapeDtypeStruct((b, n_heads, s, HEAD_DIM), BF16),
            jax.ShapeDtypeStruct((b, wab_t.shape[0], s), F32),
            jax.ShapeDtypeStruct((b, 3 * n_heads, s, HEAD_DIM), BF16),
            jax.ShapeDtypeStruct((b, s, wbr.shape[1]), BF16))
    return pl.pallas_call(
        _in_proj_kernel,
        grid=(b, s // tm),
        in_specs=[row(d), vec, vec, _resident((1, d)), _resident(wgq.shape), _resident(wgg.shape),
                  _resident(wab_t.shape), _resident(waq.shape), _resident(wbr.shape),
                  _resident(conv_w.shape)],
        out_specs=(heads(3 * n_heads), heads(n_heads),
                   pl.BlockSpec((1, wab_t.shape[0], tm), lambda i, j: (i, 0, j)),
                   heads(3 * n_heads), row(wbr.shape[1])),
        out_shape=outs,
        scratch_shapes=[pltpu.VMEM((2, TAIL + tm, IN_PROJ_COLS), F32),
                        pltpu.VMEM((wgq.shape[1] // IN_PROJ_COLS, TAIL, IN_PROJ_COLS), F32)],
        compiler_params=_params("parallel", "arbitrary"),
        name="in_proj",
    )(x, scale, shift, gain, wgq, wgg, wab_t, waq, wbr, conv_w)


GATE_ROWS = 512
COL_G, COL_BETA, COL_EG, COL_BEG, COL_EGL, COL_CD = 0, 8, 16, 24, 32, 40


def _gates_kernel(abt_ref, alog_ref, dtb_ref, rowg_ref, col_ref):
    n_heads = alog_ref.shape[0]
    r = abt_ref.shape[2]
    ab = abt_ref[0]
    a, bl = ab[:n_heads], ab[n_heads:]
    z = a + dtb_ref[...]
    softplus = jnp.maximum(z, 0.0) + jnp.log1p(jnp.exp(-jnp.abs(z)))
    g = -jnp.exp(alog_ref[...]) * softplus
    beta = jax.nn.sigmoid(bl)
    m = lax.broadcasted_iota(jnp.int32, (r, r), 0)
    j = lax.broadcasted_iota(jnp.int32, (r, r), 1)
    same = (m // CHUNK) == (j // CHUNK)
    upper = jnp.where(same & (m <= j), 1.0, 0.0).astype(F32)
    ones = jnp.where(same, 1.0, 0.0).astype(F32)
    hi = lax.Precision.HIGHEST
    gc = jnp.dot(g, upper, preferred_element_type=F32, precision=hi)
    gl = jnp.dot(g, ones, preferred_element_type=F32, precision=hi)
    m2 = lax.broadcasted_iota(jnp.int32, (r, 2 * r), 0)
    j2 = lax.broadcasted_iota(jnp.int32, (r, 2 * r), 1)
    upper2 = jnp.where(((m2 // CHUNK) == (j2 // LANES)) & ((m2 % CHUNK) <= (j2 % CHUNK)), 1.0, 0.0)
    rowg_ref[0] = jnp.dot(g, upper2.astype(F32), preferred_element_type=F32, precision=hi)
    eg = jnp.exp(gc)
    pack = jnp.concatenate(
        [gc, beta, eg, beta * eg, jnp.exp(gl - gc), jnp.exp(gl),
         jnp.zeros((LANES - 6 * n_heads, r), F32)], axis=0)
    for t in range(r // LANES):
        col_ref[0, t * LANES:(t + 1) * LANES, :] = pack[:, t * LANES:(t + 1) * LANES].T


def _gates(abt, a_log, dt_bias):
    b, h2, s = abt.shape
    n_heads = h2 // 2
    r = min(GATE_ROWS, s)
    return pl.pallas_call(
        _gates_kernel,
        grid=(b, s // r),
        in_specs=[pl.BlockSpec((1, h2, r), lambda i, j: (i, 0, j)),
                  pl.BlockSpec((n_heads, 1), lambda i, j: (0, 0)),
                  pl.BlockSpec((n_heads, 1), lambda i, j: (0, 0))],
        out_specs=(pl.BlockSpec((1, n_heads, 2 * r), lambda i, j: (i, 0, j)),
                   pl.BlockSpec((1, r, LANES), lambda i, j: (i, j, 0))),
        out_shape=(jax.ShapeDtypeStruct((b, n_heads, 2 * s), F32),
                   jax.ShapeDtypeStruct((b, s, LANES), F32)),
        compiler_params=_params("parallel", "parallel"),
        name="gdn_gates",
    )(abt, a_log.reshape(n_heads, 1), dt_bias.reshape(n_heads, 1))


GDN_ROWS = 512
GDN_HEADS_PER_PASS = 2


def _gdn_kernel(qkv_ref, gate_ref, col_ref, rowg_ref, on_ref, o_ref,
                state_ref, u_ref, wq_ref, kd_ref, at_ref):
    n_heads = gate_ref.shape[1]
    r = gate_ref.shape[2]
    n_chunks = r // CHUNK

    @pl.when(pl.program_id(1) == 0)
    def _():
        state_ref[...] = jnp.zeros_like(state_ref)

    row = lax.broadcasted_iota(jnp.int32, (CHUNK, LANES), 0)
    lane = lax.broadcasted_iota(jnp.int32, (CHUNK, LANES), 1)
    jl = lane % CHUNK
    low = lane < CHUNK
    eye_hi = jnp.where(lane - CHUNK == row, 1.0, 0.0).astype(F32)
    zeros_k = jnp.zeros((CHUNK, HEAD_DIM), BF16)
    zeros_rhs = jnp.zeros((CHUNK, 2 * HEAD_DIM), BF16)
    col_lane = lax.broadcasted_iota(jnp.int32, (r, LANES), 1)

    def solve_heads(hp, carry):
        cols = col_ref[0]

        def column(idx):
            return jnp.sum(jnp.where(col_lane == idx, cols, 0.0), axis=-1, keepdims=True)

        chains = []
        for hh in range(GDN_HEADS_PER_PASS):
            h = hp * GDN_HEADS_PER_PASS + hh
            q = qkv_ref[0, h].astype(F32)
            k = qkv_ref[0, n_heads + h].astype(F32)
            v = qkv_ref[0, 2 * n_heads + h].astype(F32)
            g_col, beta, eg, beg, egl = (column(COL_G + h), column(COL_BETA + h), column(COL_EG + h),
                                         column(COL_BEG + h), column(COL_EGL + h))
            kb = (k * beta).astype(BF16)
            rhs = jnp.concatenate([v * beta, k * beg], axis=1).astype(BF16)
            kd_ref[h] = (k * egl).astype(BF16)
            qd = (q * eg).astype(BF16)
            q16 = q.astype(BF16)
            k16 = k.astype(BF16)
            g_rows = rowg_ref[0, pl.ds(h, 1), :]
            for c in range(n_chunks):
                rows = slice(c * CHUNK, (c + 1) * CHUNK)
                sc = lax.dot_general(jnp.concatenate([q16[rows], kb[rows]], axis=0),
                                     jnp.concatenate([k16[rows], zeros_k], axis=0),
                                     NT_DIMS, preferred_element_type=F32)
                diff = g_col[rows] - g_rows[:, 2 * c * CHUNK:(2 * c + 2) * CHUNK]
                dec = jnp.exp(jnp.where(row >= jl, diff, -jnp.inf))
                at_ref[h, rows, :] = (sc[:CHUNK] * dec).astype(BF16)
                a_mat = jnp.where(row > jl, sc[CHUNK:] * dec, 0.0)
                chains.append((h, c, rhs[rows], qd[rows], jnp.where(low, -a_mat, eye_hi)))

        tbs = [ch[4] for ch in chains]
        for _ in range(int(math.log2(CHUNK))):
            nxt = []
            for tb in tbs:
                hi = tb.astype(BF16)
                lo = (tb - hi.astype(F32)).astype(BF16)
                p1 = jnp.dot(hi[:, :CHUNK], jnp.concatenate([hi, lo], axis=1), preferred_element_type=F32)
                p2 = jnp.dot(lo[:, :CHUNK], hi, preferred_element_type=F32)
                upd = p1[:, :LANES] + p1[:, LANES:] + p2
                nxt.append(jnp.where(low, upd, tb + upd))
            tbs = nxt
        for (h, c, rhs_c, qd_c, _), tb in zip(chains, tbs):
            rows = slice(c * CHUNK, (c + 1) * CHUNK)
            uw = jnp.dot(tb.astype(BF16), jnp.concatenate([zeros_rhs, rhs_c], axis=0),
                         preferred_element_type=F32)
            u_ref[h, rows, :] = uw[:, :HEAD_DIM]
            wq_ref[h, c, 0:CHUNK, :] = uw[:, HEAD_DIM:].astype(BF16)
            wq_ref[h, c, CHUNK:2 * CHUNK, :] = qd_c
        return carry

    lax.fori_loop(0, n_heads // GDN_HEADS_PER_PASS, solve_heads, 0)

    cols = col_ref[0]
    for c in range(n_chunks):
        rows = slice(c * CHUNK, (c + 1) * CHUNK)
        ws_qs = [jnp.dot(wq_ref[h, c], state_ref[h].astype(BF16), preferred_element_type=F32)
                 for h in range(n_heads)]
        v_new = [(u_ref[h, rows, :] - ws_qs[h][:CHUNK]).astype(BF16) for h in range(n_heads)]
        for h in range(n_heads):
            o = ws_qs[h][CHUNK:] + jnp.dot(at_ref[h, rows, :][:, :CHUNK], v_new[h],
                                           preferred_element_type=F32)
            cd = cols[c * CHUNK:c * CHUNK + 1, COL_CD + h:COL_CD + h + 1]
            state_ref[h] = state_ref[h] * cd + lax.dot_general(kd_ref[h, rows, :], v_new[h], TN_DIMS,
                                                               preferred_element_type=F32)
            o = o * lax.rsqrt(jnp.mean(o * o, axis=-1, keepdims=True) + EPS) * on_ref[...]
            o_ref[0, h, rows, :] = (o * _silu(gate_ref[0, h, rows, :].astype(F32))).astype(o_ref.dtype)


def _gdn(zg_qkv, zg_gate, col_tab, row_g, o_norm):
    b, n_heads, s, _ = zg_gate.shape
    r = min(GDN_ROWS, s)
    assert n_heads % GDN_HEADS_PER_PASS == 0
    heads = lambda n: pl.BlockSpec((1, n, r, HEAD_DIM), lambda i, j: (i, 0, j, 0))
    return pl.pallas_call(
        _gdn_kernel,
        grid=(b, s // r),
        in_specs=[heads(3 * n_heads), heads(n_heads),
                  pl.BlockSpec((1, r, LANES), lambda i, j: (i, j, 0)),
                  pl.BlockSpec((1, n_heads, 2 * r), lambda i, j: (i, 0, j)),
                  pl.BlockSpec((1, HEAD_DIM), lambda i, j: (0, 0))],
        out_specs=heads(n_heads),
        out_shape=jax.ShapeDtypeStruct((b, n_heads, s, HEAD_DIM), BF16),
        scratch_shapes=[pltpu.VMEM((n_heads, HEAD_DIM, HEAD_DIM), F32),
                        pltpu.VMEM((n_heads, r, HEAD_DIM), F32),
                        pltpu.VMEM((n_heads, r // CHUNK, 2 * CHUNK, HEAD_DIM), BF16),
                        pltpu.VMEM((n_heads, r, HEAD_DIM), BF16),
                        pltpu.VMEM((n_heads, r, LANES), BF16)],
        compiler_params=_params("parallel", "arbitrary"),
        name="gdn",
    )(zg_qkv, zg_gate, col_tab, row_g, o_norm.reshape(1, HEAD_DIM))


ATT_ROWS = 1024
ATT_SUB = 2 * CHUNK
ATT_KEYS = ATT_SUB + PAST_CHUNKS * CHUNK
ATT_WRAP = ATT_KEYS + ATT_SUB


def _band_bias(rel_bias):
    far = rel_bias[..., -1:]
    n_far = PAST_CHUNKS * CHUNK - MAX_REL_DIST
    near = rel_bias[..., -(ATT_KEYS - n_far):][..., ::-1]
    w = jnp.concatenate([jnp.broadcast_to(far, far.shape[:-1] + (n_far,)), near,
                         jnp.broadcast_to(far, far.shape[:-1] + (ATT_WRAP - n_far - near.shape[-1],))],
                        axis=-1)
    lead = w.shape[:-1]
    tiled = jnp.tile(w, (1,) * len(lead) + (ATT_SUB,))[..., :ATT_SUB * (ATT_WRAP - 1)]
    toep = tiled.reshape(lead + (ATT_SUB, ATT_WRAP - 1))[..., :ATT_KEYS]
    r = jnp.arange(ATT_SUB)[:, None]
    m = jnp.arange(ATT_KEYS)[None, :]
    band = m - (r // CHUNK) * CHUNK
    inside = (band >= 0) & (band < (PAST_CHUNKS + 1) * CHUNK)
    return jnp.where(inside, toep.astype(F32), NEG_BIG)


def _attn_kernel(q_ref, kp_ref, kc_ref, vp_ref, vc_ref, bias_ref, o_ref):
    first = pl.program_id(2) == 0
    rq = q_ref.shape[2]
    k_all = jnp.concatenate([kp_ref[0, 0], kc_ref[0, 0]], axis=0)
    v_all = jnp.concatenate([vp_ref[0, 0], vc_ref[0, 0]], axis=0)
    key_pos = lax.broadcasted_iota(jnp.int32, (ATT_SUB, ATT_KEYS), 1)
    scale = HEAD_DIM ** -0.5
    n_tiles = rq // ATT_SUB
    k0 = [rq - PAST_CHUNKS * CHUNK + t * ATT_SUB for t in range(n_tiles)]
    scores = [lax.dot_general(q_ref[0, 0, t * ATT_SUB:(t + 1) * ATT_SUB, :], k_all[k0[t]:k0[t] + ATT_KEYS],
                              NT_DIMS, preferred_element_type=F32) for t in range(n_tiles)]
    probs, denoms = [], []
    for t in range(n_tiles):
        s = scores[t] * scale + bias_ref[0, 0]
        s = jnp.where(first & (key_pos < rq - k0[t]), NEG_BIG, s)
        p = jnp.exp(s - jnp.max(s, axis=-1, keepdims=True))
        denoms.append(jnp.sum(p, axis=-1, keepdims=True))
        probs.append(p.astype(BF16))
    outs = [jnp.dot(probs[t], v_all[k0[t]:k0[t] + ATT_KEYS], preferred_element_type=F32)
            for t in range(n_tiles)]
    for t in range(n_tiles):
        o_ref[0, 0, t * ATT_SUB:(t + 1) * ATT_SUB, :] = (outs[t] / denoms[t]).astype(o_ref.dtype)


def _attention(za_qkv, bias_tab, layer):
    b, h3, s, _ = za_qkv.shape
    n_heads = h3 // 3
    rq = min(ATT_ROWS, s)
    assert rq >= PAST_CHUNKS * CHUNK and rq % ATT_SUB == 0
    cur = lambda off: pl.BlockSpec((1, 1, rq, HEAD_DIM), lambda i, h, j: (i, off * n_heads + h, j, 0))
    prev = lambda off: pl.BlockSpec((1, 1, rq, HEAD_DIM),
                                    lambda i, h, j: (i, off * n_heads + h, jnp.maximum(j - 1, 0), 0))
    return pl.pallas_call(
        _attn_kernel,
        grid=(b, n_heads, s // rq),
        in_specs=[cur(0), prev(1), cur(1), prev(2), cur(2),
                  pl.BlockSpec((1, 1, ATT_SUB, ATT_KEYS), lambda i, h, j: (layer, h, 0, 0))],
        out_specs=pl.BlockSpec((1, 1, rq, HEAD_DIM), lambda i, h, j: (i, h, j, 0)),
        out_shape=jax.ShapeDtypeStruct((b, n_heads, s, HEAD_DIM), BF16),
        compiler_params=_params("parallel", "parallel", "arbitrary"),
        name="band_attn",
    )(za_qkv, za_qkv, za_qkv, za_qkv, za_qkv, bias_tab)


POST_ROWS = 512
FF_COLS = 1024


def _post_kernel(x_ref, oa_ref, ob_ref, br_ref, g1_ref, sh_ref, sc_ref, g2_ref, gain_ref,
                 wo_ref, w1_ref, w2_ref, fin_ref, o_ref, *, final):
    d = x_ref.shape[2]
    n_heads = oa_ref.shape[1]
    oa = jnp.concatenate([oa_ref[0, h] for h in range(n_heads)], axis=1).astype(F32)
    ob = jnp.concatenate([ob_ref[0, h] for h in range(n_heads)], axis=1).astype(F32)
    ga = jax.nn.sigmoid(br_ref[0, :, :d].astype(F32))
    gb = jax.nn.sigmoid(br_ref[0, :, d:].astype(F32))
    merged = ga * oa + gb * ob
    x = x_ref[0] + g1_ref[0] * jnp.dot(merged.astype(BF16), wo_ref[...], preferred_element_type=F32)
    h = _rms_modulate(x, gain_ref[...], sc_ref[0], sh_ref[0]).astype(BF16)
    acc = None
    for c0 in range(0, w1_ref.shape[1], FF_COLS):
        a = jnp.dot(h, w1_ref[:, c0:c0 + FF_COLS], preferred_element_type=F32)
        a = jnp.square(jnp.maximum(a, 0.0)).astype(BF16)
        part = jnp.dot(a, w2_ref[c0:c0 + FF_COLS, :], preferred_element_type=F32)
        acc = part if acc is None else acc + part
    x = x + g2_ref[0] * acc
    if final:
        x = x * lax.rsqrt(jnp.mean(x * x, axis=-1, keepdims=True) + EPS) * fin_ref[...]
    o_ref[0] = x


def _post(x, o_a, o_b, br, gate1, shift2, scale2, gate2, gain, w_out, w1, w2, fin_gain, final):
    b, s, d = x.shape
    n_heads = d // HEAD_DIM
    tm = min(POST_ROWS, s)
    row = lambda n: pl.BlockSpec((1, tm, n), lambda i, j: (i, j, 0))
    heads = pl.BlockSpec((1, n_heads, tm, HEAD_DIM), lambda i, j: (i, 0, j, 0))
    vec = pl.BlockSpec((1, 1, d), lambda i, j: (i, 0, 0))
    return pl.pallas_call(
        functools.partial(_post_kernel, final=final),
        grid=(b, s // tm),
        in_specs=[row(d), heads, heads, row(2 * d), vec, vec, vec, vec, _resident((1, d)),
                  _resident(w_out.shape), _resident(w1.shape), _resident(w2.shape), _resident((1, d))],
        out_specs=row(d),
        out_shape=jax.ShapeDtypeStruct((b, s, d), F32),
        compiler_params=_params("parallel", "parallel"),
        name="post_final" if final else "post",
    )(x, o_a, o_b, br, gate1, shift2, scale2, gate2, gain, w_out, w1, w2, fin_gain)


def kernel(x, c, w_ada, b_ada, norm_mix, norm_mlp, w_in, conv_w, a_log, dt_bias, gdn_norm, rel_bias,
           w_out, w_ff_in, w_ff_out, final_norm):
    b, s, d = x.shape
    n_layers = w_ada.shape[0]
    n_heads = d // HEAD_DIM
    assert s % CHUNK == 0 and d % HEAD_DIM == 0

    mod = _ada_table(c, w_ada, b_ada)[:, :b]
    bias_tab = _band_bias(rel_bias)
    o_gate, o_a, o_att, o_br = 3 * d, 4 * d, 4 * d + 2 * n_heads, 7 * d + 2 * n_heads

    for l in range(n_layers):
        sh1, sc1, gt1, sh2, sc2, gt2 = (m[:, None, :] for m in jnp.split(mod[l], 6, axis=-1))
        wl = w_in[l]
        zg_qkv, zg_gate, abt, za_qkv, z_br = _in_proj(
            x, sc1, sh1, norm_mix[l].reshape(1, d),
            wl[:, :o_gate].astype(BF16), wl[:, o_gate:o_a].astype(BF16),
            wl[:, o_a:o_att].T.astype(BF16), wl[:, o_att:o_br].astype(BF16),
            wl[:, o_br:].astype(BF16), conv_w[l])
        row_g, col_tab = _gates(abt, a_log[l], dt_bias[l])
        oa = _gdn(zg_qkv, zg_gate, col_tab, row_g, gdn_norm[l])
        ob = _attention(za_qkv, bias_tab, l)
        x = _post(x, oa, ob, z_br, gt1, sh2, sc2, gt2, norm_mlp[l].reshape(1, d),
                  w_out[l].astype(BF16), w_ff_in[l].astype(BF16), w_ff_out[l].astype(BF16),
                  final_norm.reshape(1, d), final=(l == n_layers - 1))
    return x
```

```python
import functools
import math

import jax
import jax.numpy as jnp
from jax import lax
from jax.experimental import pallas as pl
from jax.experimental.pallas import tpu as pltpu

LANES = 128
SUBLANES = 8
CHUNK = 64
HEAD_DIM = 128
CONV_K = 4
PAST_CHUNKS = 8
MAX_REL_DIST = 256
EPS = 1e-6
VMEM_LIMIT_BYTES = 56 * 1024 * 1024
NEG_BIG = -1e30
LOG2E = math.log2(math.e)
ATT_Q_SCALE = HEAD_DIM ** -0.5 * LOG2E
assert CONV_K == 4

F32 = jnp.float32
BF16 = jnp.bfloat16
NT_DIMS = (((1,), (1,)), ((), ()))
TN_DIMS = (((0,), (0,)), ((), ()))


def _params(*semantics):
    return pltpu.CompilerParams(dimension_semantics=semantics,
                                vmem_limit_bytes=VMEM_LIMIT_BYTES)


def _resident(shape):
    return pl.BlockSpec(shape, lambda *_: (0,) * len(shape), pipeline_mode=pl.Buffered(1))


def _dot(a, b):
    return jnp.dot(a.astype(BF16), b.astype(BF16), preferred_element_type=F32)


def _silu(x):
    return x * jax.nn.sigmoid(x)


def _rms_modulate(x, gain, scale, shift):
    y = x * lax.rsqrt(jnp.mean(x * x, axis=-1, keepdims=True) + EPS)
    return (y * gain) * (1.0 + scale) + shift


def _ada_kernel(c_ref, w_ref, b_ref, o_ref):
    c = c_ref[...]
    o_ref[0] = jnp.dot(_silu(c), w_ref[0], preferred_element_type=F32,
                       precision=lax.Precision.HIGHEST) + b_ref[0]


def _ada_table(c, w_ada, b_ada):
    n_layers, d, d6 = w_ada.shape
    rows = -(-c.shape[0] // SUBLANES) * SUBLANES
    c_pad = jnp.zeros((rows, d), F32).at[:c.shape[0]].set(c)
    tn = d6 // 4
    return pl.pallas_call(
        _ada_kernel,
        grid=(n_layers, d6 // tn),
        in_specs=[pl.BlockSpec((rows, d), lambda l, j: (0, 0)),
                  pl.BlockSpec((1, d, tn), lambda l, j: (l, 0, j)),
                  pl.BlockSpec((1, 1, tn), lambda l, j: (l, 0, j))],
        out_specs=pl.BlockSpec((1, rows, tn), lambda l, j: (l, 0, j)),
        out_shape=jax.ShapeDtypeStruct((n_layers, rows, d6), F32),
        compiler_params=_params("parallel", "parallel"),
        name="ada_table",
    )(c_pad, w_ada, b_ada.reshape(n_layers, 1, d6))


IN_PROJ_ROWS = 512
IN_PROJ_COLS = 512
TAIL = SUBLANES
CONV_ROWS = 128


def _in_proj_kernel(x_ref, sc_ref, sh_ref, g_ref, wgq_ref, wgg_ref, wab_ref, waq_ref, wbr_ref, cw_ref,
                    gq_ref, gg_ref, abt_ref, aq_ref, br_ref, zc_ref, tail_ref):
    tm = x_ref.shape[1]
    d = x_ref.shape[2]
    heads_per_chunk = IN_PROJ_COLS // HEAD_DIM

    @pl.when(pl.program_id(1) == 0)
    def _():
        tail_ref[...] = jnp.zeros_like(tail_ref)

    h = _rms_modulate(x_ref[0], g_ref[...], sc_ref[0], sh_ref[0]).astype(BF16)

    def conv_epilogue(zc, ci, c0):
        zc[0:TAIL, :] = tail_ref[ci]
        tail_ref[ci] = zc[tm:tm + TAIL, :]
        for i in range(heads_per_chunk):
            lanes = slice(i * HEAD_DIM, (i + 1) * HEAD_DIM)
            w_lanes = slice(c0 + i * HEAD_DIM, c0 + (i + 1) * HEAD_DIM)
            for r0 in range(0, tm, CONV_ROWS):
                slab = zc[r0:r0 + CONV_ROWS + TAIL, lanes]
                prev = pltpu.roll(slab, 1, axis=0)
                w = [cw_ref[j:j + 1, w_lanes] for j in range(CONV_K)]
                near = slab * w[3] + prev * w[2]
                far = slab * w[1] + prev * w[0]
                t = _silu((near + pltpu.roll(far, 2, axis=0))[TAIL:, :])
                if c0 < 2 * d:
                    norm = lax.rsqrt(jnp.sum(t * t, axis=-1, keepdims=True) + EPS)
                    t = t * (norm * (HEAD_DIM ** -0.5) if c0 < d else norm)
                gq_ref[0, ci * heads_per_chunk + i, r0:r0 + CONV_ROWS, :] = t.astype(gq_ref.dtype)

    def heads_epilogue(zc, o_ref, ci):
        for i in range(heads_per_chunk):
            z = zc[TAIL:TAIL + tm, i * HEAD_DIM:(i + 1) * HEAD_DIM]
            if o_ref is aq_ref and ci * heads_per_chunk + i < d // HEAD_DIM:
                z = z * ATT_Q_SCALE
            o_ref[0, ci * heads_per_chunk + i] = z.astype(o_ref.dtype)

    def rows_epilogue(zc, c0):
        br_ref[0, :, c0:c0 + IN_PROJ_COLS] = zc[TAIL:TAIL + tm, :].astype(br_ref.dtype)

    heavy = [(wgq_ref, c0, functools.partial(conv_epilogue, ci=ci, c0=c0))
             for ci, c0 in enumerate(range(0, wgq_ref.shape[1], IN_PROJ_COLS))]
    light = [(w_ref, c0, functools.partial(heads_epilogue, o_ref=o_ref, ci=ci))
             for w_ref, o_ref in ((wgg_ref, gg_ref), (waq_ref, aq_ref))
             for ci, c0 in enumerate(range(0, w_ref.shape[1], IN_PROJ_COLS))]
    light += [(wbr_ref, c0, functools.partial(rows_epilogue, c0=c0))
              for c0 in range(0, wbr_ref.shape[1], IN_PROJ_COLS)]
    jobs = [job for pair in zip(heavy, light) for job in pair] + light[len(heavy):]

    def product(k):
        w_ref, c0, _ = jobs[k]
        zc_ref[k % 2, TAIL:TAIL + tm, :] = jnp.dot(h, w_ref[:, c0:c0 + IN_PROJ_COLS],
                                                   preferred_element_type=F32)

    product(0)
    for k in range(len(jobs)):
        if k + 1 < len(jobs):
            product(k + 1)
        jobs[k][2](zc_ref.at[k % 2])
    abt_ref[0] = lax.dot_general(wab_ref[...], h, NT_DIMS, preferred_element_type=F32)


def _in_proj(x, scale, shift, gain, wgq, wgg, wab_t, waq, wbr, conv_w):
    b, s, d = x.shape
    tm = min(IN_PROJ_ROWS, s)
    n_heads = d // HEAD_DIM
    row = lambda n: pl.BlockSpec((1, tm, n), lambda i, j: (i, j, 0))
    heads = lambda n: pl.BlockSpec((1, n, tm, HEAD_DIM), lambda i, j: (i, 0, j, 0))
    vec = pl.BlockSpec((1, 1, d), lambda i, j: (i, 0, 0))
    outs = (jax.ShapeDtypeStruct((b, 3 * n_heads, s, HEAD_DIM), BF16),
            jax.ShapeDtypeStruct((b, n_heads, s, HEAD_DIM), BF16),
            jax.ShapeDtypeStruct((b, wab_t.shape[0], s), F32),
            jax.ShapeDtypeStruct((b, 3 * n_heads, s, HEAD_DIM), BF16),
            jax.ShapeDtypeStruct((b, s, wbr.shape[1]), BF16))
    return pl.pallas_call(
        _in_proj_kernel,
        grid=(b, s // tm),
        in_specs=[row(d), vec, vec, _resident((1, d)), _resident(wgq.shape), _resident(wgg.shape),
                  _resident(wab_t.shape), _resident(waq.shape), _resident(wbr.shape),
                  _resident(conv_w.shape)],
        out_specs=(heads(3 * n_heads), heads(n_heads),
                   pl.BlockSpec((1, wab_t.shape[0], tm), lambda i, j: (i, 0, j)),
                   heads(3 * n_heads), row(wbr.shape[1])),
        out_shape=outs,
        scratch_shapes=[pltpu.VMEM((2, TAIL + tm, IN_PROJ_COLS), F32),
                        pltpu.VMEM((wgq.shape[1] // IN_PROJ_COLS, TAIL, IN_PROJ_COLS), F32)],
        compiler_params=_params("parallel", "arbitrary"),
        name="in_proj",
    )(x, scale, shift, gain, wgq, wgg, wab_t, waq, wbr, conv_w)


GATE_ROWS = 512
COL_G, COL_BETA, COL_EG, COL_BEG, COL_EGL, COL_CD = 0, 8, 16, 24, 32, 40


def _gates_kernel(abt_ref, alog_ref, dtb_ref, rowg_ref, col_ref):
    n_heads = alog_ref.shape[0]
    r = abt_ref.shape[2]
    ab = abt_ref[0]
    a, bl = ab[:n_heads], ab[n_heads:]
    z = a + dtb_ref[...]
    softplus = jnp.maximum(z, 0.0) + jnp.log1p(jnp.exp(-jnp.abs(z)))
    g = -jnp.exp(alog_ref[...]) * softplus
    beta = jax.nn.sigmoid(bl)
    m = lax.broadcasted_iota(jnp.int32, (r, r), 0)
    j = lax.broadcasted_iota(jnp.int32, (r, r), 1)
    same = (m // CHUNK) == (j // CHUNK)
    upper = jnp.where(same & (m <= j), 1.0, 0.0).astype(F32)
    ones = jnp.where(same, 1.0, 0.0).astype(F32)
    hi = lax.Precision.HIGHEST
    gc = jnp.dot(g, upper, preferred_element_type=F32, precision=hi)
    gl = jnp.dot(g, ones, preferred_element_type=F32, precision=hi)
    m2 = lax.broadcasted_iota(jnp.int32, (r, 2 * r), 0)
    j2 = lax.broadcasted_iota(jnp.int32, (r, 2 * r), 1)
    upper2 = jnp.where(((m2 // CHUNK) == (j2 // LANES)) & ((m2 % CHUNK) <= (j2 % CHUNK)), 1.0, 0.0)
    rowg_ref[0] = jnp.dot(g, upper2.astype(F32), preferred_element_type=F32, precision=hi)
    eg = jnp.exp(gc)
    pack = jnp.concatenate(
        [gc, beta, eg, beta * eg, jnp.exp(gl - gc), jnp.exp(gl),
         jnp.zeros((LANES - 6 * n_heads, r), F32)], axis=0)
    for t in range(r // LANES):
        col_ref[0, t * LANES:(t + 1) * LANES, :] = pack[:, t * LANES:(t + 1) * LANES].T


def _gates(abt, a_log, dt_bias):
    b, h2, s = abt.shape
    n_heads = h2 // 2
    r = min(GATE_ROWS, s)
    return pl.pallas_call(
        _gates_kernel,
        grid=(b, s // r),
        in_specs=[pl.BlockSpec((1, h2, r), lambda i, j: (i, 0, j)),
                  pl.BlockSpec((n_heads, 1), lambda i, j: (0, 0)),
                  pl.BlockSpec((n_heads, 1), lambda i, j: (0, 0))],
        out_specs=(pl.BlockSpec((1, n_heads, 2 * r), lambda i, j: (i, 0, j)),
                   pl.BlockSpec((1, r, LANES), lambda i, j: (i, j, 0))),
        out_shape=(jax.ShapeDtypeStruct((b, n_heads, 2 * s), F32),
                   jax.ShapeDtypeStruct((b, s, LANES), F32)),
        compiler_params=_params("parallel", "parallel"),
        name="gdn_gates",
    )(abt, a_log.reshape(n_heads, 1), dt_bias.reshape(n_heads, 1))


GDN_ROWS = 512
GDN_HEADS_PER_PASS = 2


def _gdn_kernel(qkv_ref, gate_ref, col_ref, rowg_ref, on_ref, o_ref,
                state_ref, u_ref, wq_ref, kd_ref, at_ref):
    n_heads = gate_ref.shape[1]
    r = gate_ref.shape[2]
    n_chunks = r // CHUNK

    @pl.when(pl.program_id(1) == 0)
    def _():
        state_ref[...] = jnp.zeros_like(state_ref)

    row = lax.broadcasted_iota(jnp.int32, (CHUNK, LANES), 0)
    lane = lax.broadcasted_iota(jnp.int32, (CHUNK, LANES), 1)
    jl = lane % CHUNK
    low = lane < CHUNK
    eye_hi = jnp.where(lane - CHUNK == row, 1.0, 0.0).astype(F32)
    zeros_k = jnp.zeros((CHUNK, HEAD_DIM), BF16)
    zeros_rhs = jnp.zeros((CHUNK, 2 * HEAD_DIM), BF16)
    col_lane = lax.broadcasted_iota(jnp.int32, (r, LANES), 1)

    def solve_heads(hp, carry):
        cols = col_ref[0]

        def column(idx):
            return jnp.sum(jnp.where(col_lane == idx, cols, 0.0), axis=-1, keepdims=True)

        chains = []
        for hh in range(GDN_HEADS_PER_PASS):
            h = hp * GDN_HEADS_PER_PASS + hh
            q = qkv_ref[0, h].astype(F32)
            k = qkv_ref[0, n_heads + h].astype(F32)
            v = qkv_ref[0, 2 * n_heads + h].astype(F32)
            g_col, beta, eg, beg, egl = (column(COL_G + h), column(COL_BETA + h), column(COL_EG + h),
                                         column(COL_BEG + h), column(COL_EGL + h))
            kb = (k * beta).astype(BF16)
            rhs = jnp.concatenate([v * beta, k * beg], axis=1).astype(BF16)
            kd_ref[h] = (k * egl).astype(BF16)
            qd = (q * eg).astype(BF16)
            q16 = q.astype(BF16)
            k16 = k.astype(BF16)
            g_rows = rowg_ref[0, pl.ds(h, 1), :]
            for c in range(n_chunks):
                rows = slice(c * CHUNK, (c + 1) * CHUNK)
                sc = lax.dot_general(jnp.concatenate([q16[rows], kb[rows]], axis=0),
                                     jnp.concatenate([k16[rows], zeros_k], axis=0),
                                     NT_DIMS, preferred_element_type=F32)
                diff = g_col[rows] - g_rows[:, 2 * c * CHUNK:(2 * c + 2) * CHUNK]
                dec = jnp.exp(jnp.where(row >= jl, diff, -jnp.inf))
                at_ref[h, rows, :] = (sc[:CHUNK] * dec).astype(BF16)
                a_mat = jnp.where(row > jl, sc[CHUNK:] * dec, 0.0)
                chains.append((h, c, rhs[rows], qd[rows], jnp.where(low, -a_mat, eye_hi)))

        tbs = [ch[4] for ch in chains]
        for _ in range(int(math.log2(CHUNK))):
            nxt = []
            for tb in tbs:
                hi = tb.astype(BF16)
                lo = (tb - hi.astype(F32)).astype(BF16)
                p1 = jnp.dot(hi[:, :CHUNK], jnp.concatenate([hi, lo], axis=1), preferred_element_type=F32)
                p2 = jnp.dot(lo[:, :CHUNK], hi, preferred_element_type=F32)
                upd = p1[:, :LANES] + p1[:, LANES:] + p2
                nxt.append(jnp.where(low, upd, tb + upd))
            tbs = nxt
        for (h, c, rhs_c, qd_c, _), tb in zip(chains, tbs):
            rows = slice(c * CHUNK, (c + 1) * CHUNK)
            uw = jnp.dot(tb.astype(BF16), jnp.concatenate([zeros_rhs, rhs_c], axis=0),
                         preferred_element_type=F32)
            u_ref[h, rows, :] = uw[:, :HEAD_DIM]
            wq_ref[h, c, 0:CHUNK, :] = uw[:, HEAD_DIM:].astype(BF16)
            wq_ref[h, c, CHUNK:2 * CHUNK, :] = qd_c
        return carry

    lax.fori_loop(0, n_heads // GDN_HEADS_PER_PASS, solve_heads, 0)

    cols = col_ref[0]
    for c in range(n_chunks):
        rows = slice(c * CHUNK, (c + 1) * CHUNK)
        ws_qs = [jnp.dot(wq_ref[h, c], state_ref[h].astype(BF16), preferred_element_type=F32)
                 for h in range(n_heads)]
        v_new = [(u_ref[h, rows, :] - ws_qs[h][:CHUNK]).astype(BF16) for h in range(n_heads)]
        for h in range(n_heads):
            o = ws_qs[h][CHUNK:] + jnp.dot(at_ref[h, rows, :][:, :CHUNK], v_new[h],
                                           preferred_element_type=F32)
            cd = cols[c * CHUNK:c * CHUNK + 1, COL_CD + h:COL_CD + h + 1]
            state_ref[h] = state_ref[h] * cd + lax.dot_general(kd_ref[h, rows, :], v_new[h], TN_DIMS,
                                                               preferred_element_type=F32)
            o = o * lax.rsqrt(jnp.mean(o * o, axis=-1, keepdims=True) + EPS) * on_ref[...]
            o_ref[0, h, rows, :] = (o * _silu(gate_ref[0, h, rows, :].astype(F32))).astype(o_ref.dtype)


def _gdn(zg_qkv, zg_gate, col_tab, row_g, o_norm):
    b, n_heads, s, _ = zg_gate.shape
    r = min(GDN_ROWS, s)
    assert n_heads % GDN_HEADS_PER_PASS == 0
    heads = lambda n: pl.BlockSpec((1, n, r, HEAD_DIM), lambda i, j: (i, 0, j, 0))
    return pl.pallas_call(
        _gdn_kernel,
        grid=(b, s // r),
        in_specs=[heads(3 * n_heads), heads(n_heads),
                  pl.BlockSpec((1, r, LANES), lambda i, j: (i, j, 0)),
                  pl.BlockSpec((1, n_heads, 2 * r), lambda i, j: (i, 0, j)),
                  pl.BlockSpec((1, HEAD_DIM), lambda i, j: (0, 0))],
        out_specs=heads(n_heads),
        out_shape=jax.ShapeDtypeStruct((b, n_heads, s, HEAD_DIM), BF16),
        scratch_shapes=[pltpu.VMEM((n_heads, HEAD_DIM, HEAD_DIM), F32),
                        pltpu.VMEM((n_heads, r, HEAD_DIM), F32),
                        pltpu.VMEM((n_heads, r // CHUNK, 2 * CHUNK, HEAD_DIM), BF16),
                        pltpu.VMEM((n_heads, r, HEAD_DIM), BF16),
                        pltpu.VMEM((n_heads, r, LANES), BF16)],
        compiler_params=_params("parallel", "arbitrary"),
        name="gdn",
    )(zg_qkv, zg_gate, col_tab, row_g, o_norm.reshape(1, HEAD_DIM))


ATT_ROWS = 1024
ATT_SUB = 2 * CHUNK
ATT_KEYS = ATT_SUB + PAST_CHUNKS * CHUNK
ATT_WRAP = ATT_KEYS + ATT_SUB


def _band_bias(rel_bias):
    far = rel_bias[..., -1:]
    n_far = PAST_CHUNKS * CHUNK - MAX_REL_DIST
    near = rel_bias[..., -(ATT_KEYS - n_far):][..., ::-1]
    w = jnp.concatenate([jnp.broadcast_to(far, far.shape[:-1] + (n_far,)), near,
                         jnp.broadcast_to(far, far.shape[:-1] + (ATT_WRAP - n_far - near.shape[-1],))],
                        axis=-1)
    lead = w.shape[:-1]
    tiled = jnp.tile(w, (1,) * len(lead) + (ATT_SUB,))[..., :ATT_SUB * (ATT_WRAP - 1)]
    toep = tiled.reshape(lead + (ATT_SUB, ATT_WRAP - 1))[..., :ATT_KEYS]
    r = jnp.arange(ATT_SUB)[:, None]
    m = jnp.arange(ATT_KEYS)[None, :]
    band = m - (r // CHUNK) * CHUNK
    inside = (band >= 0) & (band < (PAST_CHUNKS + 1) * CHUNK)
    return jnp.where(inside, toep.astype(F32) * LOG2E, NEG_BIG)


def _attn_kernel(q_ref, kp_ref, kc_ref, vp_ref, vc_ref, bias_ref, o_ref):
    first = pl.program_id(2) == 0
    rq = q_ref.shape[2]
    k_all = jnp.concatenate([kp_ref[0, 0], kc_ref[0, 0]], axis=0)
    v_all = jnp.concatenate([vp_ref[0, 0], vc_ref[0, 0]], axis=0)
    key_pos = lax.broadcasted_iota(jnp.int32, (ATT_SUB, ATT_KEYS), 1)
    n_tiles = rq // ATT_SUB
    k0 = [rq - PAST_CHUNKS * CHUNK + t * ATT_SUB for t in range(n_tiles)]
    scores = [lax.dot_general(q_ref[0, 0, t * ATT_SUB:(t + 1) * ATT_SUB, :], k_all[k0[t]:k0[t] + ATT_KEYS],
                              NT_DIMS, preferred_element_type=F32) for t in range(n_tiles)]
    probs, denoms = [], []
    for t in range(n_tiles):
        s = scores[t] + bias_ref[0, 0]
        s = jnp.where(first & (key_pos < rq - k0[t]), NEG_BIG, s)
        p = jnp.exp2(s - jnp.max(s, axis=-1, keepdims=True))
        denoms.append(jnp.sum(p, axis=-1, keepdims=True))
        probs.append(p.astype(BF16))
    outs = [jnp.dot(probs[t], v_all[k0[t]:k0[t] + ATT_KEYS], preferred_element_type=F32)
            for t in range(n_tiles)]
    for t in range(n_tiles):
        o_ref[0, 0, t * ATT_SUB:(t + 1) * ATT_SUB, :] = (outs[t] / denoms[t]).astype(o_ref.dtype)


def _attention(za_qkv, bias_tab, layer):
    b, h3, s, _ = za_qkv.shape
    n_heads = h3 // 3
    rq = min(ATT_ROWS, s)
    assert rq >= PAST_CHUNKS * CHUNK and rq % ATT_SUB == 0
    cur = lambda off: pl.BlockSpec((1, 1, rq, HEAD_DIM), lambda i, h, j: (i, off * n_heads + h, j, 0))
    prev = lambda off: pl.BlockSpec((1, 1, rq, HEAD_DIM),
                                    lambda i, h, j: (i, off * n_heads + h, jnp.maximum(j - 1, 0), 0))
    return pl.pallas_call(
        _attn_kernel,
        grid=(b, n_heads, s // rq),
        in_specs=[cur(0), prev(1), cur(1), prev(2), cur(2),
                  pl.BlockSpec((1, 1, ATT_SUB, ATT_KEYS), lambda i, h, j: (layer, h, 0, 0))],
        out_specs=pl.BlockSpec((1, 1, rq, HEAD_DIM), lambda i, h, j: (i, h, j, 0)),
        out_shape=jax.ShapeDtypeStruct((b, n_heads, s, HEAD_DIM), BF16),
        compiler_params=_params("parallel", "parallel", "arbitrary"),
        name="band_attn",
    )(za_qkv, za_qkv, za_qkv, za_qkv, za_qkv, bias_tab)


POST_ROWS = 512
FF_COLS = 1024


def _post_kernel(x_ref, oa_ref, ob_ref, br_ref, g1_ref, sh_ref, sc_ref, g2_ref, gain_ref,
                 wo_ref, w1_ref, w2_ref, fin_ref, o_ref, *, final):
    d = x_ref.shape[2]
    n_heads = oa_ref.shape[1]
    oa = jnp.concatenate([oa_ref[0, h] for h in range(n_heads)], axis=1).astype(F32)
    ob = jnp.concatenate([ob_ref[0, h] for h in range(n_heads)], axis=1).astype(F32)
    ga = jax.nn.sigmoid(br_ref[0, :, :d].astype(F32))
    gb = jax.nn.sigmoid(br_ref[0, :, d:].astype(F32))
    merged = ga * oa + gb * ob
    x = x_ref[0] + g1_ref[0] * jnp.dot(merged.astype(BF16), wo_ref[...], preferred_element_type=F32)
    h = _rms_modulate(x, gain_ref[...], sc_ref[0], sh_ref[0]).astype(BF16)
    acc = None
    for c0 in range(0, w1_ref.shape[1], FF_COLS):
        a = jnp.dot(h, w1_ref[:, c0:c0 + FF_COLS], preferred_element_type=F32)
        a = jnp.square(jnp.maximum(a, 0.0)).astype(BF16)
        part = jnp.dot(a, w2_ref[c0:c0 + FF_COLS, :], preferred_element_type=F32)
        acc = part if acc is None else acc + part
    x = x + g2_ref[0] * acc
    if final:
        x = x * lax.rsqrt(jnp.mean(x * x, axis=-1, keepdims=True) + EPS) * fin_ref[...]
    o_ref[0] = x


def _post(x, o_a, o_b, br, gate1, shift2, scale2, gate2, gain, w_out, w1, w2, fin_gain, final):
    b, s, d = x.shape
    n_heads = d // HEAD_DIM
    tm = min(POST_ROWS, s)
    row = lambda n: pl.BlockSpec((1, tm, n), lambda i, j: (i, j, 0))
    heads = pl.BlockSpec((1, n_heads, tm, HEAD_DIM), lambda i, j: (i, 0, j, 0))
    vec = pl.BlockSpec((1, 1, d), lambda i, j: (i, 0, 0))
    return pl.pallas_call(
        functools.partial(_post_kernel, final=final),
        grid=(b, s // tm),
        in_specs=[row(d), heads, heads, row(2 * d), vec, vec, vec, vec, _resident((1, d)),
                  _resident(w_out.shape), _resident(w1.shape), _resident(w2.shape), _resident((1, d))],
        out_specs=row(d),
        out_shape=jax.ShapeDtypeStruct((b, s, d), F32),
        compiler_params=_params("parallel", "parallel"),
        name="post_final" if final else "post",
    )(x, o_a, o_b, br, gate1, shift2, scale2, gate2, gain, w_out, w1, w2, fin_gain)


def kernel(x, c, w_ada, b_ada, norm_mix, norm_mlp, w_in, conv_w, a_log, dt_bias, gdn_norm, rel_bias,
           w_out, w_ff_in, w_ff_out, final_norm):
    b, s, d = x.shape
    n_layers = w_ada.shape[0]
    n_heads = d // HEAD_DIM
    assert s % CHUNK == 0 and d % HEAD_DIM == 0

    mod = _ada_table(c, w_ada, b_ada)[:, :b]
    bias_tab = _band_bias(rel_bias)
    o_gate, o_a, o_att, o_br = 3 * d, 4 * d, 4 * d + 2 * n_heads, 7 * d + 2 * n_heads

    for l in range(n_layers):
        sh1, sc1, gt1, sh2, sc2, gt2 = (m[:, None, :] for m in jnp.split(mod[l], 6, axis=-1))
        wl = w_in[l]
        zg_qkv, zg_gate, abt, za_qkv, z_br = _in_proj(
            x, sc1, sh1, norm_mix[l].reshape(1, d),
            wl[:, :o_gate].astype(BF16), wl[:, o_gate:o_a].astype(BF16),
            wl[:, o_a:o_att].T.astype(BF16), wl[:, o_att:o_br].astype(BF16),
            wl[:, o_br:].astype(BF16), conv_w[l])
        row_g, col_tab = _gates(abt, a_log[l], dt_bias[l])
        oa = _gdn(zg_qkv, zg_gate, col_tab, row_g, gdn_norm[l])
        ob = _attention(za_qkv, bias_tab, l)
        x = _post(x, oa, ob, z_br, gt1, sh2, sc2, gt2, norm_mlp[l].reshape(1, d),
                  w_out[l].astype(BF16), w_ff_in[l].astype(BF16), w_ff_out[l].astype(BF16),
                  final_norm.reshape(1, d), final=(l == n_layers - 1))
    return x
```

```python
import functools
import math

import jax
import jax.numpy as jnp
from jax import lax
from jax.experimental import pallas as pl
from jax.experimental.pallas import tpu as pltpu

LANES = 128
SUBLANES = 8
CHUNK = 64
HEAD_DIM = 128
CONV_K = 4
PAST_CHUNKS = 8
MAX_REL_DIST = 256
EPS = 1e-6
VMEM_LIMIT_BYTES = 56 * 1024 * 1024
NEG_BIG = -1e30
LOG2E = math.log2(math.e)
ATT_Q_SCALE = HEAD_DIM ** -0.5 * LOG2E
assert CONV_K == 4

F32 = jnp.float32
BF16 = jnp.bfloat16
NT_DIMS = (((1,), (1,)), ((), ()))
TN_DIMS = (((0,), (0,)), ((), ()))


def _params(*semantics):
    return pltpu.CompilerParams(dimension_semantics=semantics,
                                vmem_limit_bytes=VMEM_LIMIT_BYTES)


def _resident(shape):
    return pl.BlockSpec(shape, lambda *_: (0,) * len(shape), pipeline_mode=pl.Buffered(1))


def _dot(a, b):
    return jnp.dot(a.astype(BF16), b.astype(BF16), preferred_element_type=F32)


def _silu(x):
    return x * jax.nn.sigmoid(x)


def _rms_modulate(x, gain, scale, shift):
    y = x * lax.rsqrt(jnp.mean(x * x, axis=-1, keepdims=True) + EPS)
    return (y * gain) * (1.0 + scale) + shift


def _ada_kernel(c_ref, w_ref, b_ref, o_ref):
    c = c_ref[...]
    o_ref[0] = jnp.dot(_silu(c), w_ref[0], preferred_element_type=F32,
                       precision=lax.Precision.HIGHEST) + b_ref[0]


def _ada_table(c, w_ada, b_ada):
    n_layers, d, d6 = w_ada.shape
    rows = -(-c.shape[0] // SUBLANES) * SUBLANES
    c_pad = jnp.zeros((rows, d), F32).at[:c.shape[0]].set(c)
    tn = d6 // 4
    return pl.pallas_call(
        _ada_kernel,
        grid=(n_layers, d6 // tn),
        in_specs=[pl.BlockSpec((rows, d), lambda l, j: (0, 0)),
                  pl.BlockSpec((1, d, tn), lambda l, j: (l, 0, j)),
                  pl.BlockSpec((1, 1, tn), lambda l, j: (l, 0, j))],
        out_specs=pl.BlockSpec((1, rows, tn), lambda l, j: (l, 0, j)),
        out_shape=jax.ShapeDtypeStruct((n_layers, rows, d6), F32),
        compiler_params=_params("parallel", "parallel"),
        name="ada_table",
    )(c_pad, w_ada, b_ada.reshape(n_layers, 1, d6))


IN_PROJ_ROWS = 512
IN_PROJ_COLS = 256
TAIL = SUBLANES
CONV_ROWS = 128


def _in_proj_kernel(x_ref, sc_ref, sh_ref, g_ref, wgq_ref, wgg_ref, wab_ref, waq_ref, wbr_ref, cw_ref,
                    gq_ref, gg_ref, abt_ref, aq_ref, br_ref, zc_ref, tail_ref):
    tm = x_ref.shape[1]
    d = x_ref.shape[2]
    heads_per_chunk = IN_PROJ_COLS // HEAD_DIM

    @pl.when(pl.program_id(1) == 0)
    def _():
        tail_ref[...] = jnp.zeros_like(tail_ref)

    h = _rms_modulate(x_ref[0], g_ref[...], sc_ref[0], sh_ref[0]).astype(BF16)

    def conv_epilogue(zc, ci, c0):
        zc[0:TAIL, :] = tail_ref[ci]
        tail_ref[ci] = zc[tm:tm + TAIL, :]
        for i in range(heads_per_chunk):
            lanes = slice(i * HEAD_DIM, (i + 1) * HEAD_DIM)
            w_lanes = slice(c0 + i * HEAD_DIM, c0 + (i + 1) * HEAD_DIM)
            for r0 in range(0, tm, CONV_ROWS):
                slab = zc[r0:r0 + CONV_ROWS + TAIL, lanes]
                prev = pltpu.roll(slab, 1, axis=0)
                w = [cw_ref[j:j + 1, w_lanes] for j in range(CONV_K)]
                near = slab * w[3] + prev * w[2]
                far = slab * w[1] + prev * w[0]
                t = _silu((near + pltpu.roll(far, 2, axis=0))[TAIL:, :])
                if c0 < 2 * d:
                    norm = lax.rsqrt(jnp.sum(t * t, axis=-1, keepdims=True) + EPS)
                    t = t * (norm * (HEAD_DIM ** -0.5) if c0 < d else norm)
                gq_ref[0, ci * heads_per_chunk + i, r0:r0 + CONV_ROWS, :] = t.astype(gq_ref.dtype)

    def heads_epilogue(zc, o_ref, ci):
        for i in range(heads_per_chunk):
            z = zc[TAIL:TAIL + tm, i * HEAD_DIM:(i + 1) * HEAD_DIM]
            if o_ref is aq_ref and ci * heads_per_chunk + i < d // HEAD_DIM:
                z = z * ATT_Q_SCALE
            o_ref[0, ci * heads_per_chunk + i] = z.astype(o_ref.dtype)

    def rows_epilogue(zc, c0):
        br_ref[0, :, c0:c0 + IN_PROJ_COLS] = zc[TAIL:TAIL + tm, :].astype(br_ref.dtype)

    heavy = [(wgq_ref, c0, functools.partial(conv_epilogue, ci=ci, c0=c0))
             for ci, c0 in enumerate(range(0, wgq_ref.shape[1], IN_PROJ_COLS))]
    light = [(w_ref, c0, functools.partial(heads_epilogue, o_ref=o_ref, ci=ci))
             for w_ref, o_ref in ((wgg_ref, gg_ref), (waq_ref, aq_ref))
             for ci, c0 in enumerate(range(0, w_ref.shape[1], IN_PROJ_COLS))]
    light += [(wbr_ref, c0, functools.partial(rows_epilogue, c0=c0))
              for c0 in range(0, wbr_ref.shape[1], IN_PROJ_COLS)]
    jobs = [job for pair in zip(heavy, light) for job in pair] + light[len(heavy):]

    def product(k):
        w_ref, c0, _ = jobs[k]
        zc_ref[k % 2, TAIL:TAIL + tm, :] = jnp.dot(h, w_ref[:, c0:c0 + IN_PROJ_COLS],
                                                   preferred_element_type=F32)

    product(0)
    for k in range(len(jobs)):
        if k + 1 < len(jobs):
            product(k + 1)
        jobs[k][2](zc_ref.at[k % 2])
    abt_ref[0] = lax.dot_general(wab_ref[...], h, NT_DIMS, preferred_element_type=F32)


def _in_proj(x, scale, shift, gain, wgq, wgg, wab_t, waq, wbr, conv_w):
    b, s, d = x.shape
    tm = min(IN_PROJ_ROWS, s)
    n_heads = d // HEAD_DIM
    row = lambda n: pl.BlockSpec((1, tm, n), lambda i, j: (i, j, 0))
    heads = lambda n: pl.BlockSpec((1, n, tm, HEAD_DIM), lambda i, j: (i, 0, j, 0))
    vec = pl.BlockSpec((1, 1, d), lambda i, j: (i, 0, 0))
    outs = (jax.ShapeDtypeStruct((b, 3 * n_heads, s, HEAD_DIM), BF16),
            jax.ShapeDtypeStruct((b, n_heads, s, HEAD_DIM), BF16),
            jax.ShapeDtypeStruct((b, wab_t.shape[0], s), F32),
            jax.ShapeDtypeStruct((b, 3 * n_heads, s, HEAD_DIM), BF16),
            jax.ShapeDtypeStruct((b, s, wbr.shape[1]), BF16))
    return pl.pallas_call(
        _in_proj_kernel,
        grid=(b, s // tm),
        in_specs=[row(d), vec, vec, _resident((1, d)), _resident(wgq.shape), _resident(wgg.shape),
                  _resident(wab_t.shape), _resident(waq.shape), _resident(wbr.shape),
                  _resident(conv_w.shape)],
        out_specs=(heads(3 * n_heads), heads(n_heads),
                   pl.BlockSpec((1, wab_t.shape[0], tm), lambda i, j: (i, 0, j)),
                   heads(3 * n_heads), row(wbr.shape[1])),
        out_shape=outs,
        scratch_shapes=[pltpu.VMEM((2, TAIL + tm, IN_PROJ_COLS), F32),
                        pltpu.VMEM((wgq.shape[1] // IN_PROJ_COLS, TAIL, IN_PROJ_COLS), F32)],
        compiler_params=_params("parallel", "arbitrary"),
        name="in_proj",
    )(x, scale, shift, gain, wgq, wgg, wab_t, waq, wbr, conv_w)


GATE_ROWS = 512
COL_G, COL_BETA, COL_EG, COL_BEG, COL_EGL, COL_CD = 0, 8, 16, 24, 32, 40


def _gates_kernel(abt_ref, alog_ref, dtb_ref, rowg_ref, col_ref):
    n_heads = alog_ref.shape[0]
    r = abt_ref.shape[2]
    ab = abt_ref[0]
    a, bl = ab[:n_heads], ab[n_heads:]
    z = a + dtb_ref[...]
    softplus = jnp.maximum(z, 0.0) + jnp.log1p(jnp.exp(-jnp.abs(z)))
    g = -jnp.exp(alog_ref[...]) * softplus
    beta = jax.nn.sigmoid(bl)
    m = lax.broadcasted_iota(jnp.int32, (r, r), 0)
    j = lax.broadcasted_iota(jnp.int32, (r, r), 1)
    same = (m // CHUNK) == (j // CHUNK)
    upper = jnp.where(same & (m <= j), 1.0, 0.0).astype(F32)
    ones = jnp.where(same, 1.0, 0.0).astype(F32)
    hi = lax.Precision.HIGHEST
    gc = jnp.dot(g, upper, preferred_element_type=F32, precision=hi)
    gl = jnp.dot(g, ones, preferred_element_type=F32, precision=hi)
    m2 = lax.broadcasted_iota(jnp.int32, (r, 2 * r), 0)
    j2 = lax.broadcasted_iota(jnp.int32, (r, 2 * r), 1)
    upper2 = jnp.where(((m2 // CHUNK) == (j2 // LANES)) & ((m2 % CHUNK) <= (j2 % CHUNK)), 1.0, 0.0)
    rowg_ref[0] = jnp.dot(g, upper2.astype(F32), preferred_element_type=F32, precision=hi)
    eg = jnp.exp(gc)
    pack = jnp.concatenate(
        [gc, beta, eg, beta * eg, jnp.exp(gl - gc), jnp.exp(gl),
         jnp.zeros((LANES - 6 * n_heads, r), F32)], axis=0)
    for t in range(r // LANES):
        col_ref[0, t * LANES:(t + 1) * LANES, :] = pack[:, t * LANES:(t + 1) * LANES].T


def _gates(abt, a_log, dt_bias):
    b, h2, s = abt.shape
    n_heads = h2 // 2
    r = min(GATE_ROWS, s)
    return pl.pallas_call(
        _gates_kernel,
        grid=(b, s // r),
        in_specs=[pl.BlockSpec((1, h2, r), lambda i, j: (i, 0, j)),
                  pl.BlockSpec((n_heads, 1), lambda i, j: (0, 0)),
                  pl.BlockSpec((n_heads, 1), lambda i, j: (0, 0))],
        out_specs=(pl.BlockSpec((1, n_heads, 2 * r), lambda i, j: (i, 0, j)),
                   pl.BlockSpec((1, r, LANES), lambda i, j: (i, j, 0))),
        out_shape=(jax.ShapeDtypeStruct((b, n_heads, 2 * s), F32),
                   jax.ShapeDtypeStruct((b, s, LANES), F32)),
        compiler_params=_params("parallel", "parallel"),
        name="gdn_gates",
    )(abt, a_log.reshape(n_heads, 1), dt_bias.reshape(n_heads, 1))


GDN_ROWS = 512
GDN_HEADS_PER_PASS = 2


def _gdn_kernel(qkv_ref, gate_ref, col_ref, rowg_ref, on_ref, o_ref,
                state_ref, u_ref, wq_ref, kd_ref, at_ref):
    n_heads = gate_ref.shape[1]
    r = gate_ref.shape[2]
    n_chunks = r // CHUNK

    @pl.when(pl.program_id(1) == 0)
    def _():
        state_ref[...] = jnp.zeros_like(state_ref)

    row = lax.broadcasted_iota(jnp.int32, (CHUNK, LANES), 0)
    lane = lax.broadcasted_iota(jnp.int32, (CHUNK, LANES), 1)
    jl = lane % CHUNK
    low = lane < CHUNK
    eye_hi = jnp.where(lane - CHUNK == row, 1.0, 0.0).astype(F32)
    zeros_k = jnp.zeros((CHUNK, HEAD_DIM), BF16)
    zeros_rhs = jnp.zeros((CHUNK, 2 * HEAD_DIM), BF16)
    col_lane = lax.broadcasted_iota(jnp.int32, (r, LANES), 1)

    def solve_heads(hp, carry):
        cols = col_ref[0]

        def column(idx):
            return jnp.sum(jnp.where(col_lane == idx, cols, 0.0), axis=-1, keepdims=True)

        chains = []
        for hh in range(GDN_HEADS_PER_PASS):
            h = hp * GDN_HEADS_PER_PASS + hh
            q = qkv_ref[0, h].astype(F32)
            k = qkv_ref[0, n_heads + h].astype(F32)
            v = qkv_ref[0, 2 * n_heads + h].astype(F32)
            g_col, beta, eg, beg, egl = (column(COL_G + h), column(COL_BETA + h), column(COL_EG + h),
                                         column(COL_BEG + h), column(COL_EGL + h))
            kb = (k * beta).astype(BF16)
            rhs = jnp.concatenate([v * beta, k * beg], axis=1).astype(BF16)
            kd_ref[h] = (k * egl).astype(BF16)
            qd = (q * eg).astype(BF16)
            q16 = q.astype(BF16)
            k16 = k.astype(BF16)
            g_rows = rowg_ref[0, pl.ds(h, 1), :]
            for c in range(n_chunks):
                rows = slice(c * CHUNK, (c + 1) * CHUNK)
                sc = lax.dot_general(jnp.concatenate([q16[rows], kb[rows]], axis=0),
                                     jnp.concatenate([k16[rows], zeros_k], axis=0),
                                     NT_DIMS, preferred_element_type=F32)
                diff = g_col[rows] - g_rows[:, 2 * c * CHUNK:(2 * c + 2) * CHUNK]
                dec = jnp.exp(jnp.where(row >= jl, diff, -jnp.inf))
                at_ref[h, rows, :] = (sc[:CHUNK] * dec).astype(BF16)
                a_mat = jnp.where(row > jl, sc[CHUNK:] * dec, 0.0)
                chains.append((h, c, rhs[rows], qd[rows], jnp.where(low, -a_mat, eye_hi)))

        tbs = [ch[4] for ch in chains]
        for _ in range(int(math.log2(CHUNK))):
            nxt = []
            for tb in tbs:
                hi = tb.astype(BF16)
                lo = (tb - hi.astype(F32)).astype(BF16)
                p1 = jnp.dot(hi[:, :CHUNK], jnp.concatenate([hi, lo], axis=1), preferred_element_type=F32)
                p2 = jnp.dot(lo[:, :CHUNK], hi, preferred_element_type=F32)
                upd = p1[:, :LANES] + p1[:, LANES:] + p2
                nxt.append(jnp.where(low, upd, tb + upd))
            tbs = nxt
        for (h, c, rhs_c, qd_c, _), tb in zip(chains, tbs):
            rows = slice(c * CHUNK, (c + 1) * CHUNK)
            uw = jnp.dot(tb.astype(BF16), jnp.concatenate([zeros_rhs, rhs_c], axis=0),
                         preferred_element_type=F32)
            u_ref[h, rows, :] = uw[:, :HEAD_DIM]
            wq_ref[h, c, 0:CHUNK, :] = uw[:, HEAD_DIM:].astype(BF16)
            wq_ref[h, c, CHUNK:2 * CHUNK, :] = qd_c
        return carry

    lax.fori_loop(0, n_heads // GDN_HEADS_PER_PASS, solve_heads, 0)

    cols = col_ref[0]
    for c in range(n_chunks):
        rows = slice(c * CHUNK, (c + 1) * CHUNK)
        ws_qs = [jnp.dot(wq_ref[h, c], state_ref[h].astype(BF16), preferred_element_type=F32)
                 for h in range(n_heads)]
        v_new = [(u_ref[h, rows, :] - ws_qs[h][:CHUNK]).astype(BF16) for h in range(n_heads)]
        for h in range(n_heads):
            o = ws_qs[h][CHUNK:] + jnp.dot(at_ref[h, rows, :][:, :CHUNK], v_new[h],
                                           preferred_element_type=F32)
            cd = cols[c * CHUNK:c * CHUNK + 1, COL_CD + h:COL_CD + h + 1]
            state_ref[h] = state_ref[h] * cd + lax.dot_general(kd_ref[h, rows, :], v_new[h], TN_DIMS,
                                                               preferred_element_type=F32)
            o = o * lax.rsqrt(jnp.mean(o * o, axis=-1, keepdims=True) + EPS) * on_ref[...]
            o_ref[0, h, rows, :] = (o * _silu(gate_ref[0, h, rows, :].astype(F32))).astype(o_ref.dtype)


def _gdn(zg_qkv, zg_gate, col_tab, row_g, o_norm):
    b, n_heads, s, _ = zg_gate.shape
    r = min(GDN_ROWS, s)
    assert n_heads % GDN_HEADS_PER_PASS == 0
    heads = lambda n: pl.BlockSpec((1, n, r, HEAD_DIM), lambda i, j: (i, 0, j, 0))
    return pl.pallas_call(
        _gdn_kernel,
        grid=(b, s // r),
        in_specs=[heads(3 * n_heads), heads(n_heads),
                  pl.BlockSpec((1, r, LANES), lambda i, j: (i, j, 0)),
                  pl.BlockSpec((1, n_heads, 2 * r), lambda i, j: (i, 0, j)),
                  pl.BlockSpec((1, HEAD_DIM), lambda i, j: (0, 0))],
        out_specs=heads(n_heads),
        out_shape=jax.ShapeDtypeStruct((b, n_heads, s, HEAD_DIM), BF16),
        scratch_shapes=[pltpu.VMEM((n_heads, HEAD_DIM, HEAD_DIM), F32),
                        pltpu.VMEM((n_heads, r, HEAD_DIM), F32),
                        pltpu.VMEM((n_heads, r // CHUNK, 2 * CHUNK, HEAD_DIM), BF16),
                        pltpu.VMEM((n_heads, r, HEAD_DIM), BF16),
                        pltpu.VMEM((n_heads, r, LANES), BF16)],
        compiler_params=_params("parallel", "arbitrary"),
        name="gdn",
    )(zg_qkv, zg_gate, col_tab, row_g, o_norm.reshape(1, HEAD_DIM))


ATT_ROWS = 2048
ATT_SUB = 2 * CHUNK
ATT_KEYS = ATT_SUB + PAST_CHUNKS * CHUNK
ATT_WRAP = ATT_KEYS + ATT_SUB


def _band_bias(rel_bias):
    far = rel_bias[..., -1:]
    n_far = PAST_CHUNKS * CHUNK - MAX_REL_DIST
    near = rel_bias[..., -(ATT_KEYS - n_far):][..., ::-1]
    w = jnp.concatenate([jnp.broadcast_to(far, far.shape[:-1] + (n_far,)), near,
                         jnp.broadcast_to(far, far.shape[:-1] + (ATT_WRAP - n_far - near.shape[-1],))],
                        axis=-1)
    lead = w.shape[:-1]
    tiled = jnp.tile(w, (1,) * len(lead) + (ATT_SUB,))[..., :ATT_SUB * (ATT_WRAP - 1)]
    toep = tiled.reshape(lead + (ATT_SUB, ATT_WRAP - 1))[..., :ATT_KEYS]
    r = jnp.arange(ATT_SUB)[:, None]
    m = jnp.arange(ATT_KEYS)[None, :]
    band = m - (r // CHUNK) * CHUNK
    inside = (band >= 0) & (band < (PAST_CHUNKS + 1) * CHUNK)
    return jnp.where(inside, toep.astype(F32) * LOG2E, NEG_BIG)


def _attn_kernel(q_ref, kp_ref, kc_ref, vp_ref, vc_ref, bias_ref, o_ref):
    first = pl.program_id(2) == 0
    rq = q_ref.shape[2]
    k_all = jnp.concatenate([kp_ref[0, 0], kc_ref[0, 0]], axis=0)
    v_all = jnp.concatenate([vp_ref[0, 0], vc_ref[0, 0]], axis=0)
    key_pos = lax.broadcasted_iota(jnp.int32, (ATT_SUB, ATT_KEYS), 1)
    n_tiles = rq // ATT_SUB
    k0 = [rq - PAST_CHUNKS * CHUNK + t * ATT_SUB for t in range(n_tiles)]
    scores = [lax.dot_general(q_ref[0, 0, t * ATT_SUB:(t + 1) * ATT_SUB, :], k_all[k0[t]:k0[t] + ATT_KEYS],
                              NT_DIMS, preferred_element_type=F32) for t in range(n_tiles)]
    probs, denoms = [], []
    for t in range(n_tiles):
        s = scores[t] + bias_ref[0, 0]
        s = jnp.where(first & (key_pos < rq - k0[t]), NEG_BIG, s)
        p = jnp.exp2(s - jnp.max(s, axis=-1, keepdims=True))
        denoms.append(jnp.sum(p, axis=-1, keepdims=True))
        probs.append(p.astype(BF16))
    outs = [jnp.dot(probs[t], v_all[k0[t]:k0[t] + ATT_KEYS], preferred_element_type=F32)
            for t in range(n_tiles)]
    for t in range(n_tiles):
        o_ref[0, 0, t * ATT_SUB:(t + 1) * ATT_SUB, :] = (outs[t] / denoms[t]).astype(o_ref.dtype)


def _attention(za_qkv, bias_tab, layer):
    b, h3, s, _ = za_qkv.shape
    n_heads = h3 // 3
    rq = min(ATT_ROWS, s)
    assert rq >= PAST_CHUNKS * CHUNK and rq % ATT_SUB == 0
    cur = lambda off: pl.BlockSpec((1, 1, rq, HEAD_DIM), lambda i, h, j: (i, off * n_heads + h, j, 0))
    prev = lambda off: pl.BlockSpec((1, 1, rq, HEAD_DIM),
                                    lambda i, h, j: (i, off * n_heads + h, jnp.maximum(j - 1, 0), 0))
    return pl.pallas_call(
        _attn_kernel,
        grid=(b, n_heads, s // rq),
        in_specs=[cur(0), prev(1), cur(1), prev(2), cur(2),
                  pl.BlockSpec((1, 1, ATT_SUB, ATT_KEYS), lambda i, h, j: (layer, h, 0, 0))],
        out_specs=pl.BlockSpec((1, 1, rq, HEAD_DIM), lambda i, h, j: (i, h, j, 0)),
        out_shape=jax.ShapeDtypeStruct((b, n_heads, s, HEAD_DIM), BF16),
        compiler_params=_params("parallel", "parallel", "arbitrary"),
        name="band_attn",
    )(za_qkv, za_qkv, za_qkv, za_qkv, za_qkv, bias_tab)


POST_ROWS = 512
FF_COLS = 1024


def _post_kernel(x_ref, oa_ref, ob_ref, br_ref, g1_ref, sh_ref, sc_ref, g2_ref, gain_ref,
                 wo_ref, w1_ref, w2_ref, fin_ref, o_ref, *, final):
    d = x_ref.shape[2]
    n_heads = oa_ref.shape[1]
    oa = jnp.concatenate([oa_ref[0, h] for h in range(n_heads)], axis=1).astype(F32)
    ob = jnp.concatenate([ob_ref[0, h] for h in range(n_heads)], axis=1).astype(F32)
    ga = jax.nn.sigmoid(br_ref[0, :, :d].astype(F32))
    gb = jax.nn.sigmoid(br_ref[0, :, d:].astype(F32))
    merged = ga * oa + gb * ob
    x = x_ref[0] + g1_ref[0] * jnp.dot(merged.astype(BF16), wo_ref[...], preferred_element_type=F32)
    h = _rms_modulate(x, gain_ref[...], sc_ref[0], sh_ref[0]).astype(BF16)
    acc = None
    for c0 in range(0, w1_ref.shape[1], FF_COLS):
        a = jnp.dot(h, w1_ref[:, c0:c0 + FF_COLS], preferred_element_type=F32)
        a = jnp.square(jnp.maximum(a, 0.0)).astype(BF16)
        part = jnp.dot(a, w2_ref[c0:c0 + FF_COLS, :], preferred_element_type=F32)
        acc = part if acc is None else acc + part
    x = x + g2_ref[0] * acc
    if final:
        x = x * lax.rsqrt(jnp.mean(x * x, axis=-1, keepdims=True) + EPS) * fin_ref[...]
    o_ref[0] = x


def _post(x, o_a, o_b, br, gate1, shift2, scale2, gate2, gain, w_out, w1, w2, fin_gain, final):
    b, s, d = x.shape
    n_heads = d // HEAD_DIM
    tm = min(POST_ROWS, s)
    row = lambda n: pl.BlockSpec((1, tm, n), lambda i, j: (i, j, 0))
    heads = pl.BlockSpec((1, n_heads, tm, HEAD_DIM), lambda i, j: (i, 0, j, 0))
    vec = pl.BlockSpec((1, 1, d), lambda i, j: (i, 0, 0))
    return pl.pallas_call(
        functools.partial(_post_kernel, final=final),
        grid=(b, s // tm),
        in_specs=[row(d), heads, heads, row(2 * d), vec, vec, vec, vec, _resident((1, d)),
                  _resident(w_out.shape), _resident(w1.shape), _resident(w2.shape), _resident((1, d))],
        out_specs=row(d),
        out_shape=jax.ShapeDtypeStruct((b, s, d), F32),
        compiler_params=_params("parallel", "parallel"),
        name="post_final" if final else "post",
    )(x, o_a, o_b, br, gate1, shift2, scale2, gate2, gain, w_out, w1, w2, fin_gain)


def kernel(x, c, w_ada, b_ada, norm_mix, norm_mlp, w_in, conv_w, a_log, dt_bias, gdn_norm, rel_bias,
           w_out, w_ff_in, w_ff_out, final_norm):
    b, s, d = x.shape
    n_layers = w_ada.shape[0]
    n_heads = d // HEAD_DIM
    assert s % CHUNK == 0 and d % HEAD_DIM == 0

    mod = _ada_table(c, w_ada, b_ada)[:, :b]
    bias_tab = _band_bias(rel_bias)
    o_gate, o_a, o_att, o_br = 3 * d, 4 * d, 4 * d + 2 * n_heads, 7 * d + 2 * n_heads

    for l in range(n_layers):
        sh1, sc1, gt1, sh2, sc2, gt2 = (m[:, None, :] for m in jnp.split(mod[l], 6, axis=-1))
        wl = w_in[l]
        zg_qkv, zg_gate, abt, za_qkv, z_br = _in_proj(
            x, sc1, sh1, norm_mix[l].reshape(1, d),
            wl[:, :o_gate].astype(BF16), wl[:, o_gate:o_a].astype(BF16),
            wl[:, o_a:o_att].T.astype(BF16), wl[:, o_att:o_br].astype(BF16),
            wl[:, o_br:].astype(BF16), conv_w[l])
        row_g, col_tab = _gates(abt, a_log[l], dt_bias[l])
        oa = _gdn(zg_qkv, zg_gate, col_tab, row_g, gdn_norm[l])
        ob = _attention(za_qkv, bias_tab, l)
        x = _post(x, oa, ob, z_br, gt1, sh2, sc2, gt2, norm_mlp[l].reshape(1, d),
                  w_out[l].astype(BF16), w_ff_in[l].astype(BF16), w_ff_out[l].astype(BF16),
                  final_norm.reshape(1, d), final=(l == n_layers - 1))
    return x
```
